```python
import jax, jax.numpy as jnp
from jax import lax
import numpy as np

D_MODEL = 2048
BATCH = 1
SEQ = 8192
DEPTH = 1

BLK = 128
EPS = 1e-6
NEG = -1e30
TINY = 1e-30
FORCE = 1e4

A_DILATIONS = (1, 4, 16)
A_WINDOWS = (128, 512, 2048)
A_GROUPS = 3
A_HEADS = 4
A_HD = 128
B_HEADS = 16
B_KV = 2
B_REP = B_HEADS // B_KV
B_HD = 64
CMP_LEN = 32
CMP_STRIDE = 16
CMP_HIDDEN = 128
SLC_LEN = 64
N_SEL = 16
WIN = 512
N_BRANCH = 2
X_HEADS = 4
X_HD = 128
N_MEM = 256
PEER_HEADS = 8
PEER_DK = 256
N_KEYS = 128
N_EXPERTS = N_KEYS * N_KEYS
PEER_TOPK = 16
PEER_V_SCALE = 0.25

A_QKV_COLS = 3 * A_GROUPS * A_HEADS * A_HD
B_Q_COLS = B_HEADS * B_HD
B_KV_COLS = 3 * 2 * B_KV * B_HD
B_GATE_COLS = B_HEADS * 3
MERGE_COLS = N_BRANCH * D_MODEL
IN_COLS = A_QKV_COLS + B_Q_COLS + B_KV_COLS + B_GATE_COLS + MERGE_COLS
SPLITS = [A_QKV_COLS, A_QKV_COLS + B_Q_COLS, A_QKV_COLS + B_Q_COLS + B_KV_COLS, A_QKV_COLS + B_Q_COLS + B_KV_COLS + B_GATE_COLS]

kernel_name = 'hybrid_dilated_nsa_peer_block'


def rmsnorm(x, g):
    xf = x.astype(jnp.float32)
    y = xf * lax.rsqrt(jnp.mean(xf * xf, axis=-1, keepdims=True) + EPS)
    return (y * g.astype(jnp.float32)).astype(x.dtype)


def alibi_slopes(n):
    return jnp.power(2.0, -8.0 * (jnp.arange(n, dtype=jnp.float32) + 1.0) / n)


def masked_probs(s, ok):
    s = jnp.where(ok, s, NEG)
    m = jnp.max(s, axis=-1, keepdims=True)
    p = jnp.where(ok, jnp.exp(s - m), 0.0)
    return p / jnp.maximum(jnp.sum(p, axis=-1, keepdims=True), TINY)


def dilated_window_attn(q, k, v, slopes, dil, steps):
    B, T, H, hd = q.shape
    L = T // dil
    Lp = -(-L // BLK) * BLK
    nb = Lp // BLK
    N = B * dil

    def strided(a):
        a = a.reshape(B, L, dil, H, hd).transpose(0, 2, 3, 1, 4).reshape(N, H, L, hd)
        return jnp.pad(a, ((0, 0), (0, 0), (0, Lp - L), (0, 0))).reshape(N, H, nb, BLK, hd)

    qb, kb, vb = strided(q), strided(k), strided(v)

    def with_prev(a):
        prev = jnp.concatenate([jnp.zeros_like(a[:, :, :1]), a[:, :, :-1]], axis=2)
        return jnp.concatenate([prev, a], axis=3)

    k2, v2 = with_prev(kb), with_prev(vb)
    s = jnp.einsum('nhbqd,nhbkd->nhbqk', qb, k2).astype(jnp.float32) * (hd ** -0.5)
    rel = jnp.arange(BLK)[:, None] + BLK - jnp.arange(2 * BLK)[None, :]
    ok = (rel >= 0) & (rel <= steps)
    ok = ok[None] & ((jnp.arange(nb)[:, None, None] > 0) | (jnp.arange(2 * BLK)[None, None, :] >= BLK))
    s = s - slopes.astype(jnp.float32)[:, None, None, None] * (rel * dil).astype(jnp.float32)
    s = jnp.where(ok, s, NEG)
    m = jnp.max(s, axis=-1, keepdims=True)
    p = jnp.exp(s - m)
    den = jnp.sum(p, axis=-1, keepdims=True)
    o = jnp.einsum('nhbqk,nhbkd->nhbqd', p, v2) / den
    lse = (m + jnp.log(den))[..., 0]
    o = o.reshape(N, H, Lp, hd)[:, :, :L].reshape(B, dil, H, L, hd).transpose(0, 3, 1, 2, 4).reshape(B, T, H, hd)
    lse = lse.reshape(N, H, Lp)[:, :, :L].reshape(B, dil, H, L).transpose(0, 3, 1, 2).reshape(B, T, H)
    return o, lse


def dilated_attention(qkv):
    B, T = qkv.shape[:2]
    slopes = alibi_slopes(A_GROUPS * A_HEADS).reshape(A_GROUPS, A_HEADS)
    outs, lses = [], []
    for g in range(A_GROUPS):
        o, l = dilated_window_attn(qkv[:, :, 0, g], qkv[:, :, 1, g], qkv[:, :, 2, g], slopes[g],
                                   A_DILATIONS[g], A_WINDOWS[g] // A_DILATIONS[g])
        outs.append(o)
        lses.append(l)
    w = jax.nn.softmax(jnp.stack(lses), axis=0)
    o = jnp.einsum('gbth,gbthd->bthd', w, jnp.stack(outs))
    return o.reshape(B, T, A_HEADS * A_HD)


def nsa_attention(q, kv, gates, w_ck1, w_ck2, pe_k, w_cv1, w_cv2, pe_v):
    B, T = q.shape[:2]
    scale = B_HD ** -0.5
    slopes = alibi_slopes(B_HEADS).reshape(B_KV, B_REP)[:, :, None, None]
    qg = q.reshape(B, T, B_KV, B_REP, B_HD).transpose(0, 2, 3, 1, 4)
    gg = gates.reshape(B, T, B_KV, B_REP, 3).transpose(0, 2, 3, 1, 4)
    kvg = kv.transpose(2, 3, 0, 4, 1, 5)
    k_c, v_c, k_s, v_s, k_w, v_w = kvg[0, 0], kvg[0, 1], kvg[1, 0], kvg[1, 1], kvg[2, 0], kvg[2, 1]

    n_cmp = (T - CMP_LEN) // CMP_STRIDE + 1
    c_idx = jnp.arange(n_cmp)[:, None] * CMP_STRIDE + jnp.arange(CMP_LEN)[None, :]

    def compress(a, pe, w1, w2):
        blocks = (a[:, :, c_idx] + pe).reshape(B, B_KV, n_cmp, CMP_LEN * B_HD)
        return jax.nn.gelu(blocks @ w1) @ w2

    kc = compress(k_c, pe_k, w_ck1, w_ck2)
    vc = compress(v_c, pe_v, w_cv1, w_cv2)
    c_end = c_idx[:, -1]
    n_slc = T // SLC_LEN
    c_start = jnp.arange(n_cmp)[:, None] * CMP_STRIDE
    s_start = jnp.arange(n_slc)[None, :] * SLC_LEN
    overlap = jnp.clip(jnp.minimum(c_start + CMP_LEN, s_start + SLC_LEN) - jnp.maximum(c_start, s_start),
                       0, None).astype(jnp.float32) / CMP_LEN
    ks_blk = k_s.reshape(B, B_KV, n_slc, SLC_LEN, B_HD)
    vs_blk = v_s.reshape(B, B_KV, n_slc, SLC_LEN, B_HD)
    nsel = min(N_SEL, n_slc)
    bi = jnp.arange(B)[:, None, None, None]
    gi = jnp.arange(B_KV)[None, :, None, None]
    kw_pad = jnp.pad(k_w, ((0, 0), (0, 0), (WIN, 0), (0, 0)))
    vw_pad = jnp.pad(v_w, ((0, 0), (0, 0), (WIN, 0), (0, 0)))
    j_slc = jnp.arange(n_slc)
    w_off = jnp.arange(BLK + WIN) - WIN

    def block(b):
        t0 = b * BLK
        t = t0 + jnp.arange(BLK)
        qb = lax.dynamic_slice_in_dim(qg, t0, BLK, axis=3)
        gb = lax.dynamic_slice_in_dim(gg, t0, BLK, axis=3)
        dist = t[:, None] - c_end[None, :]
        s = jnp.einsum('bgrqd,bgcd->bgrqc', qb, kc).astype(jnp.float32) * scale - slopes * dist.astype(jnp.float32)
        p_c = masked_probs(s, dist >= 0)
        o_c = jnp.einsum('bgrqc,bgcd->bgrqd', p_c, vc)
        imp = jnp.einsum('bgrqc,cj->bgqj', p_c, overlap)
        cur = (t // SLC_LEN)[:, None]
        forced = (j_slc == 0) | (j_slc == cur) | (j_slc == cur - 1)
        score = jnp.where(forced, FORCE, jnp.where(j_slc <= cur, imp, NEG))
        top_s, top_i = lax.top_k(score, nsel)
        ks = ks_blk[bi, gi, top_i].reshape(B, B_KV, BLK, nsel * SLC_LEN, B_HD)
        vs = vs_blk[bi, gi, top_i].reshape(B, B_KV, BLK, nsel * SLC_LEN, B_HD)
        pos = (top_i[..., None] * SLC_LEN + jnp.arange(SLC_LEN)).reshape(B, B_KV, BLK, nsel * SLC_LEN)
        d_s = t[:, None] - pos
        ok_s = (d_s >= 0) & jnp.repeat(top_s > NEG / 2, SLC_LEN, axis=-1)
        s = jnp.einsum('bgrqd,bgqkd->bgrqk', qb, ks).astype(jnp.float32) * scale - slopes * d_s[:, :, None].astype(jnp.float32)
        p_s = masked_probs(s, ok_s[:, :, None])
        o_s = jnp.einsum('bgrqk,bgqkd->bgrqd', p_s, vs)
        kw = lax.dynamic_slice_in_dim(kw_pad, t0, BLK + WIN, axis=2)
        vw = lax.dynamic_slice_in_dim(vw_pad, t0, BLK + WIN, axis=2)
        pos_w = t0 + w_off
        d_w = t[:, None] - pos_w[None, :]
        ok_w = (d_w >= 0) & (d_w < WIN) & (pos_w >= 0)[None, :]
        s = jnp.einsum('bgrqd,bgkd->bgrqk', qb, kw).astype(jnp.float32) * scale - slopes * d_w.astype(jnp.float32)
        p_w = masked_probs(s, ok_w)
        o_w = jnp.einsum('bgrqk,bgkd->bgrqd', p_w, vw)
        return gb[..., 0:1] * o_c + gb[..., 1:2] * o_s + gb[..., 2:3] * o_w

    out = lax.map(block, jnp.arange(T // BLK))
    return out.transpose(1, 0, 4, 2, 3, 5).reshape(B, T, B_HEADS * B_HD)


def memory_cross_attention(hn, mn, w_q, w_kv, w_o):
    B, T, _ = hn.shape
    q = (hn @ w_q).reshape(B, T, X_HEADS, X_HD)
    kv = (mn @ w_kv).reshape(B, mn.shape[1], 2, X_HEADS, X_HD)
    s = jnp.einsum('bthd,bmhd->bhtm', q, kv[:, :, 0]).astype(jnp.float32) * (X_HD ** -0.5)
    p = jax.nn.softmax(s, axis=-1)
    o = jnp.einsum('bhtm,bmhd->bthd', p, kv[:, :, 1]).reshape(B, T, X_HEADS * X_HD)
    return o @ w_o


def peer(xn, w_pq, sub_k1, sub_k2, u_tab, v_tab):
    B, T, D = xn.shape
    half = PEER_DK // 2
    q = (xn @ w_pq).reshape(B, T, PEER_HEADS, PEER_DK)
    s1 = jnp.einsum('bthd,kd->bthk', q[..., :half], sub_k1).astype(jnp.float32)
    s2 = jnp.einsum('bthd,kd->bthk', q[..., half:], sub_k2).astype(jnp.float32)
    v1, i1 = lax.top_k(s1, PEER_TOPK)
    v2, i2 = lax.top_k(s2, PEER_TOPK)
    cand = (v1[..., :, None] + v2[..., None, :]).reshape(B, T, PEER_HEADS, PEER_TOPK * PEER_TOPK)
    vs, ci = lax.top_k(cand, PEER_TOPK)
    e1 = jnp.take_along_axis(i1, ci // PEER_TOPK, axis=-1)
    e2 = jnp.take_along_axis(i2, ci % PEER_TOPK, axis=-1)
    eid = e1 * N_KEYS + e2
    g = jax.nn.softmax(vs, axis=-1)
    n_chunks = (B * T) // BLK
    xf = xn.reshape(n_chunks, BLK, D)
    ef = eid.reshape(n_chunks, BLK, PEER_HEADS * PEER_TOPK)
    gf = g.reshape(n_chunks, BLK, PEER_HEADS * PEER_TOPK)

    def chunk(args):
        xb, eb, gb = args
        a = jax.nn.gelu(jnp.einsum('nd,ned->ne', xb, u_tab[eb]))
        return jnp.einsum('ne,ned->nd', a * gb, v_tab[eb])

    return lax.map(chunk, (xf, ef, gf)).reshape(B, T, D)


def setup_inputs(seed: int = 0) -> dict:
    key = jax.random.key(seed)
    ks = jax.random.split(key, 25)
    L, D = DEPTH, D_MODEL

    def nrm(k, shape, scale):
        return jax.random.normal(k, shape, jnp.float32) * scale

    def gain(k, shape):
        return 1.0 + 0.01 * jax.random.normal(k, shape, jnp.float32)

    return {
        'x': nrm(ks[0], (BATCH, SEQ, D), 1.0),
        'mem': nrm(ks[1], (BATCH, N_MEM, D), 1.0),
        'norm_mix': gain(ks[2], (L, D)),
        'w_in': nrm(ks[3], (L, D, IN_COLS), D ** -0.5),
        'w_cmp_k1': nrm(ks[4], (L, CMP_LEN * B_HD, CMP_HIDDEN), (CMP_LEN * B_HD) ** -0.5),
        'w_cmp_k2': nrm(ks[5], (L, CMP_HIDDEN, B_HD), CMP_HIDDEN ** -0.5),
        'pe_cmp_k': nrm(ks[6], (L, CMP_LEN, B_HD), 0.1),
        'w_cmp_v1': nrm(ks[7], (L, CMP_LEN * B_HD, CMP_HIDDEN), (CMP_LEN * B_HD) ** -0.5),
        'w_cmp_v2': nrm(ks[8], (L, CMP_HIDDEN, B_HD), CMP_HIDDEN ** -0.5),
        'pe_cmp_v': nrm(ks[9], (L, CMP_LEN, B_HD), 0.1),
        'w_up_a': nrm(ks[10], (L, A_HEADS * A_HD, D), (A_HEADS * A_HD) ** -0.5),
        'w_up_b': nrm(ks[11], (L, B_HEADS * B_HD, D), (B_HEADS * B_HD) ** -0.5),
        'w_out': nrm(ks[12], (L, D, D), D ** -0.5),
        'norm_x': gain(ks[13], (L, D)),
        'norm_mem': gain(ks[14], (L, D)),
        'w_xq': nrm(ks[15], (L, D, X_HEADS * X_HD), D ** -0.5),
        'w_xkv': nrm(ks[16], (L, D, 2 * X_HEADS * X_HD), D ** -0.5),
        'w_xo': nrm(ks[17], (L, X_HEADS * X_HD, D), (X_HEADS * X_HD) ** -0.5),
        'norm_ffn': gain(ks[18], (L, D)),
        'w_pq': nrm(ks[19], (L, D, PEER_HEADS * PEER_DK), D ** -0.5),
        'sub_keys1': nrm(ks[20], (L, N_KEYS, PEER_DK // 2), (PEER_DK // 2) ** -0.5),
        'sub_keys2': nrm(ks[21], (L, N_KEYS, PEER_DK // 2), (PEER_DK // 2) ** -0.5),
        'expert_u': nrm(ks[22], (L, N_EXPERTS, D), D ** -0.5),
        'expert_v': nrm(ks[23], (L, N_EXPERTS, D), PEER_V_SCALE),
        'norm_final': gain(ks[24], (D,)),
    }


def reference(x, mem, norm_mix, w_in, w_cmp_k1, w_cmp_k2, pe_cmp_k, w_cmp_v1, w_cmp_v2, pe_cmp_v,
              w_up_a, w_up_b, w_out, norm_x, norm_mem, w_xq, w_xkv, w_xo, norm_ffn, w_pq,
              sub_keys1, sub_keys2, expert_u, expert_v, norm_final):
    B, T, D = x.shape
    h = x
    for l in range(DEPTH):
        n = rmsnorm(h, norm_mix[l])
        proj = n @ w_in[l]
        a_qkv, b_q, b_kv, b_gate, m_gate = jnp.split(proj, SPLITS, axis=-1)
        y_a = dilated_attention(a_qkv.reshape(B, T, 3, A_GROUPS, A_HEADS, A_HD))
        y_b = nsa_attention(b_q.reshape(B, T, B_HEADS, B_HD),
                            b_kv.reshape(B, T, 3, 2, B_KV, B_HD),
                            jax.nn.sigmoid(b_gate.reshape(B, T, B_HEADS, 3)),
                            w_cmp_k1[l], w_cmp_k2[l], pe_cmp_k[l], w_cmp_v1[l], w_cmp_v2[l], pe_cmp_v[l])
        gates = jax.nn.sigmoid(m_gate.reshape(B, T, N_BRANCH, D))
        mix = gates[:, :, 0] * (y_a @ w_up_a[l]) + gates[:, :, 1] * (y_b @ w_up_b[l])
        h = h + mix @ w_out[l]
        h = h + memory_cross_attention(rmsnorm(h, norm_x[l]), rmsnorm(mem, norm_mem[l]), w_xq[l], w_xkv[l], w_xo[l])
        h = h + peer(rmsnorm(h, norm_ffn[l]), w_pq[l], sub_keys1[l], sub_keys2[l], expert_u[l], expert_v[l])
    return rmsnorm(h, norm_final).astype(x.dtype)
```

```python
import functools

import jax
import jax.numpy as jnp
from jax import lax
from jax.experimental import pallas as pl
from jax.experimental.pallas import tpu as pltpu

F32 = jnp.float32
BF16 = jnp.bfloat16

D_MODEL = 2048
BLK = 128
EPS = 1e-6
NEG = -1e30
TINY = 1e-30
FORCE = 1e4
REMOVED = -3e38

A_DILATIONS = (1, 4, 16)
A_WINDOWS = (128, 512, 2048)
A_GROUPS = 3
A_HEADS = 4
A_HD = 128
B_HEADS = 16
B_KV = 2
B_REP = B_HEADS // B_KV
B_HD = 64
CMP_LEN = 32
CMP_STRIDE = 16
CMP_HIDDEN = 128
SLC_LEN = 64
SLC_SHIFT = 6
N_SEL = 16
WIN = 512
X_HEADS = 4
X_HD = 128
PEER_HEADS = 8
PEER_DK = 256
N_KEYS = 128
N_EXPERTS = N_KEYS * N_KEYS
PEER_TOPK = 16

A_QKV_COLS = 3 * A_GROUPS * A_HEADS * A_HD
A_OUT = A_HEADS * A_HD
B_Q_COLS = B_HEADS * B_HD
B_KV_COLS = 3 * 2 * B_KV * B_HD
B_GATE_COLS = B_HEADS * 3

COL_A = 0
COL_BQ = COL_A + A_QKV_COLS
COL_BKV = COL_BQ + B_Q_COLS
COL_BG = COL_BKV + B_KV_COLS
COL_M0 = 6656
COL_M1 = COL_M0 + D_MODEL
PROJ_COLS = COL_M1 + D_MODEL

VMEM_LIMIT = 56 * 1024 * 1024

NT_DIMS = (((1,), (1,)), ((), ()))


def _cparams(sem):
    return pltpu.CompilerParams(dimension_semantics=sem, vmem_limit_bytes=VMEM_LIMIT)


def _rms(x, g):
    ms = jnp.mean(x * x, axis=-1, keepdims=True)
    return x * lax.rsqrt(ms + EPS) * g


def _gelu(x):
    return 0.5 * x * (1.0 + jnp.tanh(0.7978845608028654 * (x + 0.044715 * (x * x * x))))


def _sigmoid(x):
    return 1.0 / (1.0 + jnp.exp(-x))


def _nmm_kernel(x_ref, g_ref, w_ref, o_ref, xn_ref):
    @pl.when(pl.program_id(1) == 0)
    def _():
        xn_ref[...] = _rms(x_ref[...].astype(F32), g_ref[...]).astype(BF16)

    o_ref[...] = jnp.dot(xn_ref[...], w_ref[...], preferred_element_type=F32).astype(o_ref.dtype)


def _norm_matmul(x, gain, w, tm, tn, out_dtype):
    t, d = x.shape
    n = w.shape[1]
    return pl.pallas_call(
        _nmm_kernel,
        grid=(t // tm, n // tn),
        in_specs=[pl.BlockSpec((tm, d), lambda i, j: (i, 0)),
                  pl.BlockSpec((1, d), lambda i, j: (0, 0)),
                  pl.BlockSpec((d, tn), lambda i, j: (0, j))],
        out_specs=pl.BlockSpec((tm, tn), lambda i, j: (i, j)),
        out_shape=jax.ShapeDtypeStruct((t, n), out_dtype),
        scratch_shapes=[pltpu.VMEM((tm, d), BF16)],
        compiler_params=_cparams(("parallel", "arbitrary")),
        name="norm_matmul",
    )(x, gain.reshape(1, d), w)


def _dil_kernel(slopes_ref, q_ref, kp_ref, ko_ref, vp_ref, vo_ref, o_ref, lse_ref, *, dil, qb, group):
    b = pl.program_id(1)
    nk = qb + BLK
    q = q_ref[...]
    k = jnp.concatenate([kp_ref[...], ko_ref[...]], axis=0)
    v = jnp.concatenate([vp_ref[...], vo_ref[...]], axis=0)
    i = lax.broadcasted_iota(jnp.int32, (qb, nk), 0)
    j = lax.broadcasted_iota(jnp.int32, (qb, nk), 1)
    rel = i + BLK - j
    jmin = jnp.where(b > 0, 0, BLK)
    ok = (rel >= 0) & (rel <= BLK) & (j >= jmin)
    dist = (rel * dil).astype(F32)
    scale = A_HD ** -0.5
    for h in range(A_HEADS):
        sl = slice(h * A_HD, (h + 1) * A_HD)
        s = lax.dot_general(q[:, sl], k[:, sl], NT_DIMS, preferred_element_type=F32) * scale
        s = s - slopes_ref[group * A_HEADS + h] * dist
        s = jnp.where(ok, s, NEG)
        m = jnp.max(s, axis=-1, keepdims=True)
        p = jnp.exp(s - m)
        den = jnp.sum(p, axis=-1, keepdims=True)
        o = jnp.dot(p.astype(BF16), v[:, sl], preferred_element_type=F32) / den
        o_ref[:, sl] = o
        lse_ref[:, sl] = jnp.broadcast_to(m + jnp.log(den), (qb, A_HD))


def _dilated_group(proj, slopes, group):
    t = proj.shape[0]
    dil = A_DILATIONS[group]
    assert A_WINDOWS[group] // dil == BLK
    l = t // dil
    qb = min(256, l)
    assert l % qb == 0 and qb % BLK == 0
    if dil == 1:
        src = proj
        stride = PROJ_COLS // A_OUT
    else:
        src = proj[:, :A_QKV_COLS].reshape(l, dil * A_QKV_COLS)
        stride = A_QKV_COLS // A_OUT
    ngh = A_GROUPS

    def col(s):
        return lambda r, b: (b, r * stride + s * ngh + group)

    def colp(s):
        return lambda r, b: (jnp.maximum(b * (qb // BLK) - 1, 0), r * stride + s * ngh + group)

    o, lse = pl.pallas_call(
        functools.partial(_dil_kernel, dil=dil, qb=qb, group=group),
        grid=(dil, l // qb),
        in_specs=[pl.BlockSpec(memory_space=pltpu.SMEM),
                  pl.BlockSpec((qb, A_OUT), col(0)),
                  pl.BlockSpec((BLK, A_OUT), colp(1)),
                  pl.BlockSpec((qb, A_OUT), col(1)),
                  pl.BlockSpec((BLK, A_OUT), colp(2)),
                  pl.BlockSpec((qb, A_OUT), col(2))],
        out_specs=[pl.BlockSpec((qb, A_OUT), lambda r, b: (b, r)),
                   pl.BlockSpec((qb, A_OUT), lambda r, b: (b, r))],
        out_shape=[jax.ShapeDtypeStruct((l, dil * A_OUT), F32),
                   jax.ShapeDtypeStruct((l, dil * A_OUT), F32)],
        compiler_params=_cparams(("parallel", "parallel")),
        name=f"dilated_attn_g{group}",
    )(slopes, src, src, src, src, src)
    return o.reshape(t, A_OUT), lse.reshape(t, A_OUT)


def _cmp_kernel(a_ref, pe_ref, w1_ref, w2_ref, o_ref):
    a = a_ref[0]
    n16 = a.shape[0]
    nxt = pltpu.roll(a, n16 - 1, axis=0)
    blocks = jnp.concatenate([a, nxt], axis=1) + pe_ref[0]
    hid = _gelu(jnp.dot(blocks.astype(BF16), w1_ref[0], preferred_element_type=F32))
    out = jnp.dot(hid.astype(BF16), w2_ref[0], preferred_element_type=F32)
    row = lax.broadcasted_iota(jnp.int32, out.shape, 0)
    o_ref[0] = jnp.where(row < n16 - 1, out, 0.0)


def _nsa_compress(proj, pe_k, pe_v, w_k1, w_k2, w_v1, w_v2):
    t = proj.shape[0]
    assert CMP_LEN == 2 * CMP_STRIDE and t % CMP_STRIDE == 0
    n16 = t // CMP_STRIDE
    kv = proj[:, COL_BKV:COL_BKV + 2 * B_KV * B_HD].astype(F32)
    a4 = kv.reshape(t, 2 * B_KV, B_HD).transpose(1, 0, 2).reshape(2 * B_KV, n16, CMP_STRIDE * B_HD)
    pe = jnp.stack([pe_k.reshape(1, -1), pe_v.reshape(1, -1)])
    w1 = jnp.stack([w_k1, w_v1]).astype(BF16)
    w2 = jnp.stack([w_k2, w_v2]).astype(BF16)
    kd = CMP_LEN * B_HD
    out = pl.pallas_call(
        _cmp_kernel,
        grid=(2 * B_KV,),
        in_specs=[pl.BlockSpec((1, n16, CMP_STRIDE * B_HD), lambda i: (i, 0, 0)),
                  pl.BlockSpec((1, 1, kd), lambda i: (i // B_KV, 0, 0)),
                  pl.BlockSpec((1, kd, CMP_HIDDEN), lambda i: (i // B_KV, 0, 0)),
                  pl.BlockSpec((1, CMP_HIDDEN, B_HD), lambda i: (i // B_KV, 0, 0))],
        out_specs=pl.BlockSpec((1, n16, B_HD), lambda i: (i, 0, 0)),
        out_shape=jax.ShapeDtypeStruct((2 * B_KV, n16, B_HD), F32),
        compiler_params=_cparams(("parallel",)),
        name="nsa_compress",
    )(a4, pe, w1, w2)
    kc = jnp.concatenate([out[0], out[1]], axis=-1).astype(BF16)
    vc = jnp.concatenate([out[2], out[3]], axis=-1).astype(BF16)
    return kc, vc


SEL_KT = 512
WIN_KEYS = WIN + BLK


def _nsa_kernel(slopes_ref, q0_ref, q1_ref, gate_ref, kc_ref, vc_ref, ks_ref, vs_ref, kw_ref, vw_ref,
                ov_ref, o_ref, qg_ref, oc_ref, ow_ref, acc_ref, m_ref, l_ref):
    b = pl.program_id(0)
    t0 = b * BLK
    rows = B_REP * BLK
    ncmp = kc_ref.shape[0]
    nslc = ov_ref.shape[1]
    lane = lax.broadcasted_iota(jnp.int32, (1, 2 * B_HD), 1)
    lo = lane < B_HD
    tq = t0 + lax.broadcasted_iota(jnp.int32, (BLK, 1), 0)
    qcat = jnp.concatenate([q0_ref[...], q1_ref[...]], axis=1)
    gates = _sigmoid(gate_ref[...].astype(F32))
    zero = jnp.zeros((), BF16)

    for g in range(B_KV):
        half = lo if g == 0 else jnp.logical_not(lo)
        for i in range(B_REP):
            qg_ref[i * BLK:(i + 1) * BLK, :] = jnp.where(half, qcat[:, i * 2 * B_HD:(i + 1) * 2 * B_HD], zero)
        qg = qg_ref[...]
        slope = [slopes_ref[g * B_REP + i] for i in range(B_REP)]

        cend = lax.broadcasted_iota(jnp.int32, (1, ncmp), 1) * CMP_STRIDE + (CMP_LEN - 1)
        ok_c = cend <= tq
        dist_c = (t0 + BLK - 1 - cend).astype(F32)
        s_all = lax.dot_general(qg, kc_ref[...], NT_DIMS, preferred_element_type=F32)
        psum = jnp.zeros((BLK, ncmp), F32)
        for i in range(B_REP):
            z = jnp.where(ok_c, s_all[i * BLK:(i + 1) * BLK] - slope[i] * dist_c, NEG)
            m = jnp.max(z, axis=-1, keepdims=True)
            p = jnp.where(ok_c, jnp.exp(z - m), 0.0)
            p = p / jnp.maximum(jnp.sum(p, axis=-1, keepdims=True), TINY)
            psum = psum + p
            oc_ref[i * BLK:(i + 1) * BLK, :] = jnp.dot(p.astype(BF16), vc_ref[...], preferred_element_type=F32)

        p_hi = psum.astype(BF16)
        p_lo = (psum - p_hi.astype(F32)).astype(BF16)
        imp = (jnp.dot(p_hi, ov_ref[...], preferred_element_type=F32)
               + jnp.dot(p_lo, ov_ref[...], preferred_element_type=F32))
        jj = lax.broadcasted_iota(jnp.int32, (1, nslc), 1)
        jf = jj.astype(F32)
        cur = jnp.right_shift(tq, SLC_SHIFT)
        forced = (jj == 0) | (jj == cur) | (jj == cur - 1)
        score = jnp.where(forced, FORCE, jnp.where(jj <= cur, imp, NEG))
        sel = jnp.zeros((BLK, nslc), F32)
        for _ in range(min(N_SEL, nslc)):
            mx = jnp.max(score, axis=-1, keepdims=True)
            idx = jnp.min(jnp.where(score == mx, jf, 1e9), axis=-1, keepdims=True)
            pick = jf == idx
            sel = jnp.where(pick & (mx > NEG / 2), 1.0, sel)
            score = jnp.where(pick, REMOVED, score)
        sel_b = sel.astype(BF16)

        wstart = pl.multiple_of(jnp.maximum(t0 - WIN, 0), BLK)
        uw = wstart + lax.broadcasted_iota(jnp.int32, (1, WIN_KEYS), 1)
        dw = tq - uw
        ok_w = (dw >= 0) & (dw < WIN)
        dist_w = (t0 + BLK - 1 - uw).astype(F32)
        kw = kw_ref[pl.ds(wstart, WIN_KEYS), :]
        vw = vw_ref[pl.ds(wstart, WIN_KEYS), :]
        s_all = lax.dot_general(qg, kw, NT_DIMS, preferred_element_type=F32)
        for i in range(B_REP):
            z = jnp.where(ok_w, s_all[i * BLK:(i + 1) * BLK] - slope[i] * dist_w, NEG)
            m = jnp.max(z, axis=-1, keepdims=True)
            p = jnp.exp(z - m)
            den = jnp.sum(p, axis=-1, keepdims=True)
            ow_ref[i * BLK:(i + 1) * BLK, :] = jnp.dot(p.astype(BF16), vw, preferred_element_type=F32) / den

        m_ref[...] = jnp.full((rows, 1), NEG, F32)
        l_ref[...] = jnp.zeros((rows, 1), F32)
        acc_ref[...] = jnp.zeros((rows, 2 * B_HD), F32)
        ntiles = (t0 + BLK + SEL_KT - 1) // SEL_KT

        def tile_body(it, carry):
            kt = ntiles - 1 - it
            k0 = pl.multiple_of(kt * SEL_KT, SEL_KT)
            us = k0 + lax.broadcasted_iota(jnp.int32, (1, SEL_KT), 1)
            jrow = lax.broadcasted_iota(jnp.int32, (nslc, SEL_KT), 0)
            expand = jnp.where(jrow == jnp.right_shift(us, SLC_SHIFT), 1.0, 0.0).astype(BF16)
            selx = jnp.dot(sel_b, expand, preferred_element_type=F32)
            ok_s = (selx > 0.5) & (us <= tq)
            dist_s = (t0 + BLK - 1 - us).astype(F32)
            ks = ks_ref[pl.ds(k0, SEL_KT), :]
            vs = vs_ref[pl.ds(k0, SEL_KT), :]
            s_t = lax.dot_general(qg_ref[...], ks, NT_DIMS, preferred_element_type=F32)
            for i in range(B_REP):
                rs = slice(i * BLK, (i + 1) * BLK)
                z = jnp.where(ok_s, s_t[rs] - slope[i] * dist_s, NEG)
                m_old = m_ref[rs, :]
                m_new = jnp.maximum(m_old, jnp.max(z, axis=-1, keepdims=True))
                alpha = jnp.exp(m_old - m_new)
                p = jnp.exp(z - m_new)
                l_ref[rs, :] = alpha * l_ref[rs, :] + jnp.sum(p, axis=-1, keepdims=True)
                acc_ref[rs, :] = alpha * acc_ref[rs, :] + jnp.dot(p.astype(BF16), vs, preferred_element_type=F32)
                m_ref[rs, :] = m_new
            return carry

        lax.fori_loop(0, ntiles, tile_body, 0)

        for i in range(B_REP):
            rs = slice(i * BLK, (i + 1) * BLK)
            h = g * B_REP + i
            o_s = acc_ref[rs, :] / l_ref[rs, :]
            out = (gates[:, 3 * h:3 * h + 1] * oc_ref[rs, :] + gates[:, 3 * h + 1:3 * h + 2] * o_s
                   + gates[:, 3 * h + 2:3 * h + 3] * ow_ref[rs, :])
            cs = slice(i * 2 * B_HD, (i + 1) * 2 * B_HD)
            if g == 0:
                o_ref[:, cs] = out.astype(o_ref.dtype)
            else:
                o_ref[:, cs] = jnp.where(lo, o_ref[:, cs], out.astype(o_ref.dtype))


def _nsa_attention(proj, kc, vc, slopes):
    t = proj.shape[0]
    assert t % SEL_KT == 0 and t % SLC_LEN == 0 and SLC_LEN == 1 << SLC_SHIFT
    ncmp = kc.shape[0]
    nslc = t // SLC_LEN
    c_start = jnp.arange(ncmp)[:, None] * CMP_STRIDE
    s_start = jnp.arange(nslc)[None, :] * SLC_LEN
    overlap = jnp.clip(jnp.minimum(c_start + CMP_LEN, s_start + SLC_LEN) - jnp.maximum(c_start, s_start),
                       0, None).astype(F32) / CMP_LEN
    overlap = jnp.where(jnp.arange(ncmp)[:, None] < ncmp - 1, overlap, 0.0).astype(BF16)
    w2 = 2 * B_HD
    qcol = COL_BQ // (4 * w2)
    kvcol = COL_BKV // w2
    full = lambda c: pl.BlockSpec((t, w2), lambda b: (0, c))
    rows = B_REP * BLK
    return pl.pallas_call(
        _nsa_kernel,
        grid=(t // BLK,),
        in_specs=[pl.BlockSpec(memory_space=pltpu.SMEM),
                  pl.BlockSpec((BLK, 4 * w2), lambda b: (b, qcol)),
                  pl.BlockSpec((BLK, 4 * w2), lambda b: (b, qcol + 1)),
                  pl.BlockSpec((BLK, w2), lambda b: (b, COL_BG // w2)),
                  pl.BlockSpec((ncmp, w2), lambda b: (0, 0)),
                  pl.BlockSpec((ncmp, w2), lambda b: (0, 0)),
                  full(kvcol + 2), full(kvcol + 3), full(kvcol + 4), full(kvcol + 5),
                  pl.BlockSpec((ncmp, nslc), lambda b: (0, 0))],
        out_specs=pl.BlockSpec((BLK, B_HEADS * B_HD), lambda b: (b, 0)),
        out_shape=jax.ShapeDtypeStruct((t, B_HEADS * B_HD), BF16),
        scratch_shapes=[pltpu.VMEM((rows, w2), BF16),
                        pltpu.VMEM((rows, w2), F32),
                        pltpu.VMEM((rows, w2), F32),
                        pltpu.VMEM((rows, w2), F32),
                        pltpu.VMEM((rows, 1), F32),
                        pltpu.VMEM((rows, 1), F32)],
        compiler_params=_cparams(("parallel",)),
        name="nsa_attention",
    )(slopes, proj, proj, proj, kc, vc, proj, proj, proj, proj, overlap)


MERGE_KC = 512


def _merge_kernel(o0_ref, o1_ref, o2_ref, l0_ref, l1_ref, l2_ref, yb_ref, m0_ref, m1_ref, x_ref,
                  wa_ref, wb_ref, wo_ref, h_ref, ya_ref):
    kc = pl.program_id(1)

    @pl.when(kc == 0)
    def _():
        l0, l1, l2 = l0_ref[...], l1_ref[...], l2_ref[...]
        m = jnp.maximum(jnp.maximum(l0, l1), l2)
        e0, e1, e2 = jnp.exp(l0 - m), jnp.exp(l1 - m), jnp.exp(l2 - m)
        den = e0 + e1 + e2
        ya = (e0 / den) * o0_ref[...] + (e1 / den) * o1_ref[...] + (e2 / den) * o2_ref[...]
        ya_ref[...] = ya.astype(BF16)
        h_ref[...] = x_ref[...]

    ua = jnp.dot(ya_ref[...], wa_ref[...], preferred_element_type=F32)
    ub = jnp.dot(yb_ref[...], wb_ref[...], preferred_element_type=F32)
    mix = _sigmoid(m0_ref[...].astype(F32)) * ua + _sigmoid(m1_ref[...].astype(F32)) * ub
    h_ref[...] += jnp.dot(mix.astype(BF16), wo_ref[...], preferred_element_type=F32)


def _merge(x, proj, dil_o, dil_lse, yb, wa, wb, wo, tm):
    t, d = x.shape
    nk = d // MERGE_KC
    row = lambda w: pl.BlockSpec((tm, w), lambda i, k: (i, 0))
    return pl.pallas_call(
        _merge_kernel,
        grid=(t // tm, nk),
        in_specs=[row(A_OUT)] * 6 + [row(B_HEADS * B_HD),
                  pl.BlockSpec((tm, MERGE_KC), lambda i, k: (i, COL_M0 // MERGE_KC + k)),
                  pl.BlockSpec((tm, MERGE_KC), lambda i, k: (i, COL_M1 // MERGE_KC + k)),
                  row(d),
                  pl.BlockSpec((A_OUT, MERGE_KC), lambda i, k: (0, k)),
                  pl.BlockSpec((B_HEADS * B_HD, MERGE_KC), lambda i, k: (0, k)),
                  pl.BlockSpec((MERGE_KC, d), lambda i, k: (k, 0))],
        out_specs=pl.BlockSpec((tm, d), lambda i, k: (i, 0)),
        out_shape=jax.ShapeDtypeStruct((t, d), F32),
        scratch_shapes=[pltpu.VMEM((tm, A_OUT), BF16)],
        compiler_params=_cparams(("parallel", "arbitrary")),
        name="merge",
    )(*dil_o, *dil_lse, yb, proj, proj, x, wa, wb, wo)


def _cross_kernel(h_ref, gx_ref, gf_ref, wq_ref, kv_ref, wo_ref, h2_ref, xn_ref):
    h = h_ref[...]
    hn = _rms(h, gx_ref[...]).astype(BF16)
    q = jnp.dot(hn, wq_ref[...], preferred_element_type=F32).astype(BF16)
    kv = kv_ref[...]
    nh = X_HEADS * X_HD
    outs = []
    for hd in range(X_HEADS):
        sl = slice(hd * X_HD, (hd + 1) * X_HD)
        s = lax.dot_general(q[:, sl], kv[:, sl], NT_DIMS, preferred_element_type=F32) * (X_HD ** -0.5)
        m = jnp.max(s, axis=-1, keepdims=True)
        p = jnp.exp(s - m)
        p = p / jnp.sum(p, axis=-1, keepdims=True)
        outs.append(jnp.dot(p.astype(BF16), kv[:, nh + hd * X_HD:nh + (hd + 1) * X_HD],
                            preferred_element_type=F32))
    o = jnp.concatenate(outs, axis=1).astype(BF16)
    h2 = h + jnp.dot(o, wo_ref[...], preferred_element_type=F32)
    h2_ref[...] = h2
    xn_ref[...] = _rms(h2, gf_ref[...]).astype(BF16)


def _cross(h1, norm_x, norm_ffn, wq, kv, wo, tm):
    t, d = h1.shape
    nm = kv.shape[0]
    nh = X_HEADS * X_HD
    const = lambda shape: pl.BlockSpec(shape, lambda i: (0, 0))
    return pl.pallas_call(
        _cross_kernel,
        grid=(t // tm,),
        in_specs=[pl.BlockSpec((tm, d), lambda i: (i, 0)), const((1, d)), const((1, d)),
                  const((d, nh)), const((nm, 2 * nh)), const((nh, d))],
        out_specs=[pl.BlockSpec((tm, d), lambda i: (i, 0)), pl.BlockSpec((tm, d), lambda i: (i, 0))],
        out_shape=[jax.ShapeDtypeStruct((t, d), F32), jax.ShapeDtypeStruct((t, d), BF16)],
        compiler_params=_cparams(("parallel",)),
        name="cross_attn",
    )(h1, norm_x.reshape(1, d), norm_ffn.reshape(1, d), wq, kv, wo)


ROUTE_TB = 128


def _top16_rows(s, rowf, nrows):
    vals, idxs = [], []
    for _ in range(PEER_TOPK):
        mx = jnp.max(s, axis=0, keepdims=True)
        idx = jnp.min(jnp.where(s == mx, rowf, float(nrows)), axis=0, keepdims=True)
        vals.append(mx)
        idxs.append(idx)
        s = jnp.where(rowf == idx, REMOVED, s)
    return vals, idxs


def _route_kernel(xn_ref, wq_ref, k1_ref, k2_ref, w_ref, q_scr, g_scr, e1_scr, e2_scr, gt_scr, e1t_scr, e2t_scr):
    tb = xn_ref.shape[0]
    half = PEER_DK // 2
    q = jnp.dot(xn_ref[...], wq_ref[...], preferred_element_type=F32).astype(BF16)
    for c in range(2 * PEER_HEADS):
        q_scr[c] = q[:, c * half:(c + 1) * half]
    rowf = lax.broadcasted_iota(jnp.int32, (N_KEYS, tb), 0).astype(F32)
    nc = PEER_TOPK * PEER_TOPK
    rowc = lax.broadcasted_iota(jnp.int32, (nc, tb), 0).astype(F32)

    def head_body(h, carry):
        s1 = lax.dot_general(k1_ref[...], q_scr[2 * h], NT_DIMS, preferred_element_type=F32)
        s2 = lax.dot_general(k2_ref[...], q_scr[2 * h + 1], NT_DIMS, preferred_element_type=F32)
        v1, i1 = _top16_rows(s1, rowf, N_KEYS)
        v2, i2 = _top16_rows(s2, rowf, N_KEYS)
        v2a = jnp.concatenate(v2, axis=0)
        i2a = jnp.concatenate(i2, axis=0)
        cand = jnp.concatenate([v1[a] + v2a for a in range(PEER_TOPK)], axis=0)
        e1c = jnp.concatenate([jnp.broadcast_to(i1[a], (PEER_TOPK, tb)) for a in range(PEER_TOPK)], axis=0)
        e2c = jnp.concatenate([i2a] * PEER_TOPK, axis=0)
        vs, e1, e2 = [], [], []
        for _ in range(PEER_TOPK):
            mx = jnp.max(cand, axis=0, keepdims=True)
            idx = jnp.min(jnp.where(cand == mx, rowc, float(nc)), axis=0, keepdims=True)
            pick = rowc == idx
            vs.append(mx)
            e1.append(jnp.sum(jnp.where(pick, e1c, 0.0), axis=0, keepdims=True))
            e2.append(jnp.sum(jnp.where(pick, e2c, 0.0), axis=0, keepdims=True))
            cand = jnp.where(pick, REMOVED, cand)
        vsa = jnp.concatenate(vs, axis=0)
        ex = jnp.exp(vsa - vs[0])
        gate = ex / jnp.sum(ex, axis=0, keepdims=True)
        r0 = pl.multiple_of(h * PEER_TOPK, PEER_TOPK)
        g_scr[pl.ds(r0, PEER_TOPK), :] = gate
        e1_scr[pl.ds(r0, PEER_TOPK), :] = jnp.concatenate(e1, axis=0)
        e2_scr[pl.ds(r0, PEER_TOPK), :] = jnp.concatenate(e2, axis=0)
        return carry

    lax.fori_loop(0, PEER_HEADS, head_body, 0)

    gt_scr[...] = g_scr[...].T
    e1t_scr[...] = e1_scr[...].T
    e2t_scr[...] = e2_scr[...].T
    sub = lax.broadcasted_iota(jnp.int32, (N_KEYS, PEER_HEADS * PEER_TOPK), 0).astype(F32)

    def tok_body(t, carry):
        shape = (N_KEYS, PEER_HEADS * PEER_TOPK)
        g_row = jnp.broadcast_to(gt_scr[pl.ds(t, 1), :], shape)
        e1_row = jnp.broadcast_to(e1t_scr[pl.ds(t, 1), :], shape)
        e2_row = jnp.broadcast_to(e2t_scr[pl.ds(t, 1), :], shape)
        a = jnp.where(e1_row == sub, g_row, 0.0).astype(BF16)
        bm = jnp.where(e2_row == sub, 1.0, 0.0).astype(BF16)
        w_ref[t] = lax.dot_general(a, bm, NT_DIMS, preferred_element_type=F32).astype(w_ref.dtype)
        return carry

    lax.fori_loop(0, tb, tok_body, 0)


def _peer_route(xn, wq, k1, k2):
    t, d = xn.shape
    tb = ROUTE_TB
    half = PEER_DK // 2
    slots = PEER_HEADS * PEER_TOPK
    const = lambda shape: pl.BlockSpec(shape, lambda i: (0,) * len(shape))
    return pl.pallas_call(
        _route_kernel,
        grid=(t // tb,),
        in_specs=[pl.BlockSpec((tb, d), lambda i: (i, 0)), const((d, PEER_HEADS * PEER_DK)),
                  const((N_KEYS, half)), const((N_KEYS, half))],
        out_specs=pl.BlockSpec((tb, N_KEYS, N_KEYS), lambda i: (i, 0, 0)),
        out_shape=jax.ShapeDtypeStruct((t, N_KEYS, N_KEYS), BF16),
        scratch_shapes=[pltpu.VMEM((2 * PEER_HEADS, tb, half), BF16),
                        pltpu.VMEM((slots, tb), F32), pltpu.VMEM((slots, tb), F32), pltpu.VMEM((slots, tb), F32),
                        pltpu.VMEM((tb, slots), F32), pltpu.VMEM((tb, slots), F32), pltpu.VMEM((tb, slots), F32)],
        compiler_params=_cparams(("parallel",)),
        name="peer_route",
    )(xn, wq, k1, k2)


def _dense_kernel(xn_ref, ut_ref, w_ref, v_ref, h_ref, gf_ref, o_ref, acc_ref):
    e = pl.program_id(1)

    @pl.when(e == 0)
    def _():
        acc_ref[...] = jnp.zeros_like(acc_ref)

    a = _gelu(jnp.dot(xn_ref[...], ut_ref[...], preferred_element_type=F32))
    aw = (a * w_ref[...].astype(F32)).astype(BF16)
    acc_ref[...] += jnp.dot(aw, v_ref[...], preferred_element_type=F32)

    @pl.when(e == pl.num_programs(1) - 1)
    def _():
        o_ref[...] = _rms(h_ref[...] + acc_ref[...], gf_ref[...]).astype(o_ref.dtype)


def _peer_dense(xn, ut, wmap, v, h2, norm_final, tb, eb):
    t, d = xn.shape
    ne = ut.shape[1]
    return pl.pallas_call(
        _dense_kernel,
        grid=(t // tb, ne // eb),
        in_specs=[pl.BlockSpec((tb, d), lambda i, e: (i, 0)),
                  pl.BlockSpec((d, eb), lambda i, e: (0, e)),
                  pl.BlockSpec((tb, eb), lambda i, e: (i, e)),
                  pl.BlockSpec((eb, d), lambda i, e: (e, 0)),
                  pl.BlockSpec((tb, d), lambda i, e: (i, 0)),
                  pl.BlockSpec((1, d), lambda i, e: (0, 0))],
        out_specs=pl.BlockSpec((tb, d), lambda i, e: (i, 0)),
        out_shape=jax.ShapeDtypeStruct((t, d), F32),
        scratch_shapes=[pltpu.VMEM((tb, d), F32)],
        compiler_params=_cparams(("parallel", "arbitrary")),
        name="peer_dense",
    )(xn, ut, wmap, v, h2, norm_final.reshape(1, d))


def _alibi_slopes(n):
    return jnp.power(2.0, -8.0 * (jnp.arange(n, dtype=F32) + 1.0) / n)


def _prep_w_in(w_in):
    d = w_in.shape[0]
    wa = w_in[:, :COL_BQ]
    wq = (w_in[:, COL_BQ:COL_BKV] * (B_HD ** -0.5)).reshape(d, B_KV, B_REP, B_HD)
    wq = wq.transpose(0, 2, 1, 3).reshape(d, B_Q_COLS)
    wkv = w_in[:, COL_BKV:COL_BG]
    wg = w_in[:, COL_BG:COL_BG + B_GATE_COLS]
    pad = jnp.zeros((d, COL_M0 - COL_BG - B_GATE_COLS), w_in.dtype)
    wm = w_in[:, COL_BG + B_GATE_COLS:]
    return jnp.concatenate([wa, wq, wkv, wg, pad, wm], axis=1).astype(BF16)


def _block(x, mem, norm_mix, w_in, w_cmp_k1, w_cmp_k2, pe_cmp_k, w_cmp_v1, w_cmp_v2, pe_cmp_v,
           w_up_a, w_up_b, w_out, norm_x, norm_mem, w_xq, w_xkv, w_xo, norm_ffn, w_pq,
           sub_keys1, sub_keys2, expert_u, expert_v, norm_out, out_dtype):
    t, d = x.shape
    assert d == D_MODEL and w_in.shape[1] == COL_BG + B_GATE_COLS + 2 * D_MODEL
    tm = min(1024, t)

    proj = _norm_matmul(x, norm_mix, _prep_w_in(w_in), tm, 512, BF16)

    slopes_a = _alibi_slopes(A_GROUPS * A_HEADS)
    dil = [_dilated_group(proj, slopes_a, g) for g in range(A_GROUPS)]

    kc, vc = _nsa_compress(proj, pe_cmp_k, pe_cmp_v, w_cmp_k1, w_cmp_k2, w_cmp_v1, w_cmp_v2)
    yb = _nsa_attention(proj, kc, vc, _alibi_slopes(B_HEADS))

    wb = w_up_b.reshape(B_KV, B_REP, B_HD, d).transpose(1, 0, 2, 3).reshape(B_HEADS * B_HD, d)
    h1 = _merge(x, proj, [o for o, _ in dil], [l for _, l in dil], yb,
                w_up_a.astype(BF16), wb.astype(BF16), w_out.astype(BF16), min(512, t))

    kvm = _norm_matmul(mem, norm_mem, w_xkv.astype(BF16), mem.shape[0], 512, BF16)
    h2, xn = _cross(h1, norm_x, norm_ffn, w_xq.astype(BF16), kvm, w_xo.astype(BF16), min(512, t))

    wmap = _peer_route(xn, w_pq.astype(BF16), sub_keys1.astype(BF16), sub_keys2.astype(BF16))
    wmap = wmap.reshape(t, N_EXPERTS)
    return _peer_dense(xn, expert_u.astype(BF16).T, wmap, expert_v.astype(BF16), h2, norm_out,
                       min(512, t), 1024).astype(out_dtype)


@jax.jit
def kernel(x, mem, norm_mix, w_in, w_cmp_k1, w_cmp_k2, pe_cmp_k, w_cmp_v1, w_cmp_v2, pe_cmp_v, w_up_a, w_up_b,
           w_out, norm_x, norm_mem, w_xq, w_xkv, w_xo, norm_ffn, w_pq, sub_keys1, sub_keys2, expert_u,
           expert_v, norm_final):
    assert x.shape[0] == 1 and mem.shape[0] == 1 and norm_mix.shape[0] == 1
    out = _block(x[0], mem[0], norm_mix[0], w_in[0], w_cmp_k1[0], w_cmp_k2[0], pe_cmp_k[0], w_cmp_v1[0],
                 w_cmp_v2[0], pe_cmp_v[0], w_up_a[0], w_up_b[0], w_out[0], norm_x[0], norm_mem[0], w_xq[0],
                 w_xkv[0], w_xo[0], norm_ffn[0], w_pq[0], sub_keys1[0], sub_keys2[0], expert_u[0],
                 expert_v[0], norm_final, x.dtype)
    return out[None]
```

```python
import functools

import jax
import jax.numpy as jnp
from jax import lax
from jax.experimental import pallas as pl
from jax.experimental.pallas import tpu as pltpu

F32 = jnp.float32
BF16 = jnp.bfloat16

D_MODEL = 2048
BLK = 128
EPS = 1e-6
NEG = -1e30
TINY = 1e-30
FORCE = 1e4
REMOVED = -3e38

A_DILATIONS = (1, 4, 16)
A_WINDOWS = (128, 512, 2048)
A_GROUPS = 3
A_HEADS = 4
A_HD = 128
B_HEADS = 16
B_KV = 2
B_REP = B_HEADS // B_KV
B_HD = 64
CMP_LEN = 32
CMP_STRIDE = 16
CMP_HIDDEN = 128
SLC_LEN = 64
SLC_SHIFT = 6
N_SEL = 16
WIN = 512
X_HEADS = 4
X_HD = 128
PEER_HEADS = 8
PEER_DK = 256
N_KEYS = 128
N_EXPERTS = N_KEYS * N_KEYS
PEER_TOPK = 16

A_QKV_COLS = 3 * A_GROUPS * A_HEADS * A_HD
A_OUT = A_HEADS * A_HD
B_Q_COLS = B_HEADS * B_HD
B_KV_COLS = 3 * 2 * B_KV * B_HD
B_GATE_COLS = B_HEADS * 3

COL_A = 0
COL_BQ = COL_A + A_QKV_COLS
COL_BKV = COL_BQ + B_Q_COLS
COL_BG = COL_BKV + B_KV_COLS
COL_M0 = 6656
COL_M1 = COL_M0 + D_MODEL
PROJ_COLS = COL_M1 + D_MODEL

VMEM_LIMIT = 56 * 1024 * 1024

NT_DIMS = (((1,), (1,)), ((), ()))


def _cparams(sem):
    return pltpu.CompilerParams(dimension_semantics=sem, vmem_limit_bytes=VMEM_LIMIT)


def _rms(x, g):
    ms = jnp.mean(x * x, axis=-1, keepdims=True)
    return x * lax.rsqrt(ms + EPS) * g


def _gelu(x):
    return 0.5 * x * (1.0 + jnp.tanh(0.7978845608028654 * (x + 0.044715 * (x * x * x))))


def _sigmoid(x):
    return 1.0 / (1.0 + jnp.exp(-x))


def _nmm_kernel(x_ref, g_ref, w_ref, o_ref, xn_ref):
    @pl.when(pl.program_id(1) == 0)
    def _():
        xn_ref[...] = _rms(x_ref[...].astype(F32), g_ref[...]).astype(BF16)

    o_ref[...] = jnp.dot(xn_ref[...], w_ref[...], preferred_element_type=F32).astype(o_ref.dtype)


def _norm_matmul(x, gain, w, tm, tn, out_dtype):
    t, d = x.shape
    n = w.shape[1]
    return pl.pallas_call(
        _nmm_kernel,
        grid=(t // tm, n // tn),
        in_specs=[pl.BlockSpec((tm, d), lambda i, j: (i, 0)),
                  pl.BlockSpec((1, d), lambda i, j: (0, 0)),
                  pl.BlockSpec((d, tn), lambda i, j: (0, j))],
        out_specs=pl.BlockSpec((tm, tn), lambda i, j: (i, j)),
        out_shape=jax.ShapeDtypeStruct((t, n), out_dtype),
        scratch_shapes=[pltpu.VMEM((tm, d), BF16)],
        compiler_params=_cparams(("parallel", "arbitrary")),
        name="norm_matmul",
    )(x, gain.reshape(1, d), w)


def _dil_kernel(slopes_ref, q_ref, kp_ref, ko_ref, vp_ref, vo_ref, o_ref, lse_ref, *, dil, qb, group):
    b = pl.program_id(1)
    nk = qb + BLK
    q = q_ref[...]
    k = jnp.concatenate([kp_ref[...], ko_ref[...]], axis=0)
    v = jnp.concatenate([vp_ref[...], vo_ref[...]], axis=0)
    i = lax.broadcasted_iota(jnp.int32, (qb, nk), 0)
    j = lax.broadcasted_iota(jnp.int32, (qb, nk), 1)
    rel = i + BLK - j
    jmin = jnp.where(b > 0, 0, BLK)
    ok = (rel >= 0) & (rel <= BLK) & (j >= jmin)
    dist = (rel * dil).astype(F32)
    scale = A_HD ** -0.5
    for h in range(A_HEADS):
        sl = slice(h * A_HD, (h + 1) * A_HD)
        s = lax.dot_general(q[:, sl], k[:, sl], NT_DIMS, preferred_element_type=F32) * scale
        s = s - slopes_ref[group * A_HEADS + h] * dist
        s = jnp.where(ok, s, NEG)
        m = jnp.max(s, axis=-1, keepdims=True)
        p = jnp.exp(s - m)
        den = jnp.sum(p, axis=-1, keepdims=True)
        o = jnp.dot(p.astype(BF16), v[:, sl], preferred_element_type=F32) / den
        o_ref[:, sl] = o
        lse_ref[:, sl] = jnp.broadcast_to(m + jnp.log(den), (qb, A_HD))


def _dilated_group(proj, slopes, group):
    t = proj.shape[0]
    dil = A_DILATIONS[group]
    assert A_WINDOWS[group] // dil == BLK
    l = t // dil
    qb = min(256, l)
    assert l % qb == 0 and qb % BLK == 0
    if dil == 1:
        src = proj
        stride = PROJ_COLS // A_OUT
    else:
        src = proj[:, :A_QKV_COLS].reshape(l, dil * A_QKV_COLS)
        stride = A_QKV_COLS // A_OUT
    ngh = A_GROUPS

    def col(s):
        return lambda r, b: (b, r * stride + s * ngh + group)

    def colp(s):
        return lambda r, b: (jnp.maximum(b * (qb // BLK) - 1, 0), r * stride + s * ngh + group)

    o, lse = pl.pallas_call(
        functools.partial(_dil_kernel, dil=dil, qb=qb, group=group),
        grid=(dil, l // qb),
        in_specs=[pl.BlockSpec(memory_space=pltpu.SMEM),
                  pl.BlockSpec((qb, A_OUT), col(0)),
                  pl.BlockSpec((BLK, A_OUT), colp(1)),
                  pl.BlockSpec((qb, A_OUT), col(1)),
                  pl.BlockSpec((BLK, A_OUT), colp(2)),
                  pl.BlockSpec((qb, A_OUT), col(2))],
        out_specs=[pl.BlockSpec((qb, A_OUT), lambda r, b: (b, r)),
                   pl.BlockSpec((qb, A_OUT), lambda r, b: (b, r))],
        out_shape=[jax.ShapeDtypeStruct((l, dil * A_OUT), F32),
                   jax.ShapeDtypeStruct((l, dil * A_OUT), F32)],
        compiler_params=_cparams(("parallel", "parallel")),
        name=f"dilated_attn_g{group}",
    )(slopes, src, src, src, src, src)
    return o.reshape(t, A_OUT), lse.reshape(t, A_OUT)


def _cmp_kernel(a_ref, pe_ref, w1_ref, w2_ref, o_ref):
    a = a_ref[0]
    n16 = a.shape[0]
    nxt = pltpu.roll(a, n16 - 1, axis=0)
    blocks = jnp.concatenate([a, nxt], axis=1) + pe_ref[0]
    hid = _gelu(jnp.dot(blocks.astype(BF16), w1_ref[0], preferred_element_type=F32))
    out = jnp.dot(hid.astype(BF16), w2_ref[0], preferred_element_type=F32)
    row = lax.broadcasted_iota(jnp.int32, out.shape, 0)
    o_ref[0] = jnp.where(row < n16 - 1, out, 0.0)


def _nsa_compress(proj, pe_k, pe_v, w_k1, w_k2, w_v1, w_v2):
    t = proj.shape[0]
    assert CMP_LEN == 2 * CMP_STRIDE and t % CMP_STRIDE == 0
    n16 = t // CMP_STRIDE
    kv = proj[:, COL_BKV:COL_BKV + 2 * B_KV * B_HD].astype(F32)
    a4 = kv.reshape(t, 2 * B_KV, B_HD).transpose(1, 0, 2).reshape(2 * B_KV, n16, CMP_STRIDE * B_HD)
    pe = jnp.stack([pe_k.reshape(1, -1), pe_v.reshape(1, -1)])
    w1 = jnp.stack([w_k1, w_v1]).astype(BF16)
    w2 = jnp.stack([w_k2, w_v2]).astype(BF16)
    kd = CMP_LEN * B_HD
    out = pl.pallas_call(
        _cmp_kernel,
        grid=(2 * B_KV,),
        in_specs=[pl.BlockSpec((1, n16, CMP_STRIDE * B_HD), lambda i: (i, 0, 0)),
                  pl.BlockSpec((1, 1, kd), lambda i: (i // B_KV, 0, 0)),
                  pl.BlockSpec((1, kd, CMP_HIDDEN), lambda i: (i // B_KV, 0, 0)),
                  pl.BlockSpec((1, CMP_HIDDEN, B_HD), lambda i: (i // B_KV, 0, 0))],
        out_specs=pl.BlockSpec((1, n16, B_HD), lambda i: (i, 0, 0)),
        out_shape=jax.ShapeDtypeStruct((2 * B_KV, n16, B_HD), F32),
        compiler_params=_cparams(("parallel",)),
        name="nsa_compress",
    )(a4, pe, w1, w2)
    kc = jnp.concatenate([out[0], out[1]], axis=-1).astype(BF16)
    vc = jnp.concatenate([out[2], out[3]], axis=-1).astype(BF16)
    return kc, vc


SEL_KT = 512
WIN_KEYS = WIN + BLK


def _nsa_kernel(slopes_ref, q0_ref, q1_ref, gate_ref, kc_ref, vc_ref, ks_ref, vs_ref, kw_ref, vw_ref,
                ov_ref, o_ref, qg_ref, oc_ref, ow_ref, acc_ref, m_ref, l_ref):
    b = pl.program_id(0)
    t0 = b * BLK
    rows = B_REP * BLK
    ncmp = kc_ref.shape[0]
    nslc = ov_ref.shape[1]
    lane = lax.broadcasted_iota(jnp.int32, (1, 2 * B_HD), 1)
    lo = lane < B_HD
    tq = t0 + lax.broadcasted_iota(jnp.int32, (BLK, 1), 0)
    qcat = jnp.concatenate([q0_ref[...], q1_ref[...]], axis=1)
    gates = _sigmoid(gate_ref[...].astype(F32))
    zero = jnp.zeros((), BF16)

    for g in range(B_KV):
        half = lo if g == 0 else jnp.logical_not(lo)
        for i in range(B_REP):
            qg_ref[i * BLK:(i + 1) * BLK, :] = jnp.where(half, qcat[:, i * 2 * B_HD:(i + 1) * 2 * B_HD], zero)
        qg = qg_ref[...]
        slope = [slopes_ref[g * B_REP + i] for i in range(B_REP)]

        cend = lax.broadcasted_iota(jnp.int32, (1, ncmp), 1) * CMP_STRIDE + (CMP_LEN - 1)
        ok_c = cend <= tq
        dist_c = (t0 + BLK - 1 - cend).astype(F32)
        s_all = lax.dot_general(qg, kc_ref[...], NT_DIMS, preferred_element_type=F32)
        psum = jnp.zeros((BLK, ncmp), F32)
        for i in range(B_REP):
            z = jnp.where(ok_c, s_all[i * BLK:(i + 1) * BLK] - slope[i] * dist_c, NEG)
            m = jnp.max(z, axis=-1, keepdims=True)
            p = jnp.where(ok_c, jnp.exp(z - m), 0.0)
            p = p / jnp.maximum(jnp.sum(p, axis=-1, keepdims=True), TINY)
            psum = psum + p
            oc_ref[i * BLK:(i + 1) * BLK, :] = jnp.dot(p.astype(BF16), vc_ref[...], preferred_element_type=F32)

        p_hi = psum.astype(BF16)
        p_lo = (psum - p_hi.astype(F32)).astype(BF16)
        imp = (jnp.dot(p_hi, ov_ref[...], preferred_element_type=F32)
               + jnp.dot(p_lo, ov_ref[...], preferred_element_type=F32))
        jj = lax.broadcasted_iota(jnp.int32, (1, nslc), 1)
        jf = jj.astype(F32)
        cur = jnp.right_shift(tq, SLC_SHIFT)
        forced = (jj == 0) | (jj == cur) | (jj == cur - 1)
        score = jnp.where(forced, FORCE, jnp.where(jj <= cur, imp, NEG))
        sel = jnp.zeros((BLK, nslc), F32)
        for _ in range(min(N_SEL, nslc)):
            mx = jnp.max(score, axis=-1, keepdims=True)
            idx = jnp.min(jnp.where(score == mx, jf, 1e9), axis=-1, keepdims=True)
            pick = jf == idx
            sel = jnp.where(pick & (mx > NEG / 2), 1.0, sel)
            score = jnp.where(pick, REMOVED, score)
        sel_b = sel.astype(BF16)

        wstart = pl.multiple_of(jnp.maximum(t0 - WIN, 0), BLK)
        uw = wstart + lax.broadcasted_iota(jnp.int32, (1, WIN_KEYS), 1)
        dw = tq - uw
        ok_w = (dw >= 0) & (dw < WIN)
        dist_w = (t0 + BLK - 1 - uw).astype(F32)
        kw = kw_ref[pl.ds(wstart, WIN_KEYS), :]
        vw = vw_ref[pl.ds(wstart, WIN_KEYS), :]
        s_all = lax.dot_general(qg, kw, NT_DIMS, preferred_element_type=F32)
        for i in range(B_REP):
            z = jnp.where(ok_w, s_all[i * BLK:(i + 1) * BLK] - slope[i] * dist_w, NEG)
            m = jnp.max(z, axis=-1, keepdims=True)
            p = jnp.exp(z - m)
            den = jnp.sum(p, axis=-1, keepdims=True)
            ow_ref[i * BLK:(i + 1) * BLK, :] = jnp.dot(p.astype(BF16), vw, preferred_element_type=F32) / den

        m_ref[...] = jnp.full((rows, 1), NEG, F32)
        l_ref[...] = jnp.zeros((rows, 1), F32)
        acc_ref[...] = jnp.zeros((rows, 2 * B_HD), F32)
        ntiles = (t0 + BLK + SEL_KT - 1) // SEL_KT

        def tile_body(it, carry):
            kt = ntiles - 1 - it
            k0 = pl.multiple_of(kt * SEL_KT, SEL_KT)
            us = k0 + lax.broadcasted_iota(jnp.int32, (1, SEL_KT), 1)
            jrow = lax.broadcasted_iota(jnp.int32, (nslc, SEL_KT), 0)
            expand = jnp.where(jrow == jnp.right_shift(us, SLC_SHIFT), 1.0, 0.0).astype(BF16)
            selx = jnp.dot(sel_b, expand, preferred_element_type=F32)
            ok_s = (selx > 0.5) & (us <= tq)
            dist_s = (t0 + BLK - 1 - us).astype(F32)
            ks = ks_ref[pl.ds(k0, SEL_KT), :]
            vs = vs_ref[pl.ds(k0, SEL_KT), :]
            s_t = lax.dot_general(qg_ref[...], ks, NT_DIMS, preferred_element_type=F32)
            for i in range(B_REP):
                rs = slice(i * BLK, (i + 1) * BLK)
                z = jnp.where(ok_s, s_t[rs] - slope[i] * dist_s, NEG)
                m_old = m_ref[rs, :]
                m_new = jnp.maximum(m_old, jnp.max(z, axis=-1, keepdims=True))
                alpha = jnp.exp(m_old - m_new)
                p = jnp.exp(z - m_new)
                l_ref[rs, :] = alpha * l_ref[rs, :] + jnp.sum(p, axis=-1, keepdims=True)
                acc_ref[rs, :] = alpha * acc_ref[rs, :] + jnp.dot(p.astype(BF16), vs, preferred_element_type=F32)
                m_ref[rs, :] = m_new
            return carry

        lax.fori_loop(0, ntiles, tile_body, 0)

        for i in range(B_REP):
            rs = slice(i * BLK, (i + 1) * BLK)
            h = g * B_REP + i
            o_s = acc_ref[rs, :] / l_ref[rs, :]
            out = (gates[:, 3 * h:3 * h + 1] * oc_ref[rs, :] + gates[:, 3 * h + 1:3 * h + 2] * o_s
                   + gates[:, 3 * h + 2:3 * h + 3] * ow_ref[rs, :])
            cs = slice(i * 2 * B_HD, (i + 1) * 2 * B_HD)
            if g == 0:
                o_ref[:, cs] = out.astype(o_ref.dtype)
            else:
                o_ref[:, cs] = jnp.where(lo, o_ref[:, cs], out.astype(o_ref.dtype))


def _nsa_attention(proj, kc, vc, slopes):
    t = proj.shape[0]
    assert t % SEL_KT == 0 and t % SLC_LEN == 0 and SLC_LEN == 1 << SLC_SHIFT
    ncmp = kc.shape[0]
    nslc = t // SLC_LEN
    c_start = jnp.arange(ncmp)[:, None] * CMP_STRIDE
    s_start = jnp.arange(nslc)[None, :] * SLC_LEN
    overlap = jnp.clip(jnp.minimum(c_start + CMP_LEN, s_start + SLC_LEN) - jnp.maximum(c_start, s_start),
                       0, None).astype(F32) / CMP_LEN
    overlap = jnp.where(jnp.arange(ncmp)[:, None] < ncmp - 1, overlap, 0.0).astype(BF16)
    w2 = 2 * B_HD
    qcol = COL_BQ // (4 * w2)
    kvcol = COL_BKV // w2
    full = lambda c: pl.BlockSpec((t, w2), lambda b: (0, c))
    rows = B_REP * BLK
    return pl.pallas_call(
        _nsa_kernel,
        grid=(t // BLK,),
        in_specs=[pl.BlockSpec(memory_space=pltpu.SMEM),
                  pl.BlockSpec((BLK, 4 * w2), lambda b: (b, qcol)),
                  pl.BlockSpec((BLK, 4 * w2), lambda b: (b, qcol + 1)),
                  pl.BlockSpec((BLK, w2), lambda b: (b, COL_BG // w2)),
                  pl.BlockSpec((ncmp, w2), lambda b: (0, 0)),
                  pl.BlockSpec((ncmp, w2), lambda b: (0, 0)),
                  full(kvcol + 2), full(kvcol + 3), full(kvcol + 4), full(kvcol + 5),
                  pl.BlockSpec((ncmp, nslc), lambda b: (0, 0))],
        out_specs=pl.BlockSpec((BLK, B_HEADS * B_HD), lambda b: (b, 0)),
        out_shape=jax.ShapeDtypeStruct((t, B_HEADS * B_HD), BF16),
        scratch_shapes=[pltpu.VMEM((rows, w2), BF16),
                        pltpu.VMEM((rows, w2), F32),
                        pltpu.VMEM((rows, w2), F32),
                        pltpu.VMEM((rows, w2), F32),
                        pltpu.VMEM((rows, 1), F32),
                        pltpu.VMEM((rows, 1), F32)],
        compiler_params=_cparams(("parallel",)),
        name="nsa_attention",
    )(slopes, proj, proj, proj, kc, vc, proj, proj, proj, proj, overlap)


MERGE_KC = 512


def _merge_kernel(o0_ref, o1_ref, o2_ref, l0_ref, l1_ref, l2_ref, yb_ref, m0_ref, m1_ref, x_ref,
                  wa_ref, wb_ref, wo_ref, h_ref, ya_ref):
    kc = pl.program_id(1)

    @pl.when(kc == 0)
    def _():
        l0, l1, l2 = l0_ref[...], l1_ref[...], l2_ref[...]
        m = jnp.maximum(jnp.maximum(l0, l1), l2)
        e0, e1, e2 = jnp.exp(l0 - m), jnp.exp(l1 - m), jnp.exp(l2 - m)
        den = e0 + e1 + e2
        ya = (e0 / den) * o0_ref[...] + (e1 / den) * o1_ref[...] + (e2 / den) * o2_ref[...]
        ya_ref[...] = ya.astype(BF16)
        h_ref[...] = x_ref[...]

    ua = jnp.dot(ya_ref[...], wa_ref[...], preferred_element_type=F32)
    ub = jnp.dot(yb_ref[...], wb_ref[...], preferred_element_type=F32)
    mix = _sigmoid(m0_ref[...].astype(F32)) * ua + _sigmoid(m1_ref[...].astype(F32)) * ub
    h_ref[...] += jnp.dot(mix.astype(BF16), wo_ref[...], preferred_element_type=F32)


def _merge(x, proj, dil_o, dil_lse, yb, wa, wb, wo, tm):
    t, d = x.shape
    nk = d // MERGE_KC
    row = lambda w: pl.BlockSpec((tm, w), lambda i, k: (i, 0))
    return pl.pallas_call(
        _merge_kernel,
        grid=(t // tm, nk),
        in_specs=[row(A_OUT)] * 6 + [row(B_HEADS * B_HD),
                  pl.BlockSpec((tm, MERGE_KC), lambda i, k: (i, COL_M0 // MERGE_KC + k)),
                  pl.BlockSpec((tm, MERGE_KC), lambda i, k: (i, COL_M1 // MERGE_KC + k)),
                  row(d),
                  pl.BlockSpec((A_OUT, MERGE_KC), lambda i, k: (0, k)),
                  pl.BlockSpec((B_HEADS * B_HD, MERGE_KC), lambda i, k: (0, k)),
                  pl.BlockSpec((MERGE_KC, d), lambda i, k: (k, 0))],
        out_specs=pl.BlockSpec((tm, d), lambda i, k: (i, 0)),
        out_shape=jax.ShapeDtypeStruct((t, d), F32),
        scratch_shapes=[pltpu.VMEM((tm, A_OUT), BF16)],
        compiler_params=_cparams(("parallel", "arbitrary")),
        name="merge",
    )(*dil_o, *dil_lse, yb, proj, proj, x, wa, wb, wo)


def _cross_kernel(h_ref, gx_ref, gf_ref, wq_ref, kv_ref, wo_ref, h2_ref, xn_ref):
    h = h_ref[...]
    hn = _rms(h, gx_ref[...]).astype(BF16)
    q = jnp.dot(hn, wq_ref[...], preferred_element_type=F32).astype(BF16)
    kv = kv_ref[...]
    nh = X_HEADS * X_HD
    outs = []
    for hd in range(X_HEADS):
        sl = slice(hd * X_HD, (hd + 1) * X_HD)
        s = lax.dot_general(q[:, sl], kv[:, sl], NT_DIMS, preferred_element_type=F32) * (X_HD ** -0.5)
        m = jnp.max(s, axis=-1, keepdims=True)
        p = jnp.exp(s - m)
        p = p / jnp.sum(p, axis=-1, keepdims=True)
        outs.append(jnp.dot(p.astype(BF16), kv[:, nh + hd * X_HD:nh + (hd + 1) * X_HD],
                            preferred_element_type=F32))
    o = jnp.concatenate(outs, axis=1).astype(BF16)
    h2 = h + jnp.dot(o, wo_ref[...], preferred_element_type=F32)
    h2_ref[...] = h2
    xn_ref[...] = _rms(h2, gf_ref[...]).astype(BF16)


def _cross(h1, norm_x, norm_ffn, wq, kv, wo, tm):
    t, d = h1.shape
    nm = kv.shape[0]
    nh = X_HEADS * X_HD
    const = lambda shape: pl.BlockSpec(shape, lambda i: (0, 0))
    return pl.pallas_call(
        _cross_kernel,
        grid=(t // tm,),
        in_specs=[pl.BlockSpec((tm, d), lambda i: (i, 0)), const((1, d)), const((1, d)),
                  const((d, nh)), const((nm, 2 * nh)), const((nh, d))],
        out_specs=[pl.BlockSpec((tm, d), lambda i: (i, 0)), pl.BlockSpec((tm, d), lambda i: (i, 0))],
        out_shape=[jax.ShapeDtypeStruct((t, d), F32), jax.ShapeDtypeStruct((t, d), BF16)],
        compiler_params=_cparams(("parallel",)),
        name="cross_attn",
    )(h1, norm_x.reshape(1, d), norm_ffn.reshape(1, d), wq, kv, wo)


ROUTE_TB = 128


def _top16_rows(s, rowf, nrows):
    vals, idxs = [], []
    for _ in range(PEER_TOPK):
        mx = jnp.max(s, axis=0, keepdims=True)
        idx = jnp.min(jnp.where(s == mx, rowf, float(nrows)), axis=0, keepdims=True)
        vals.append(mx)
        idxs.append(idx)
        s = jnp.where(rowf == idx, REMOVED, s)
    return vals, idxs


def _route_kernel(xn_ref, wq_ref, k1_ref, k2_ref, w_ref, q_scr, g_scr, e1_scr, e2_scr, gt_scr, e1t_scr, e2t_scr):
    tb = xn_ref.shape[0]
    half = PEER_DK // 2
    q = jnp.dot(xn_ref[...], wq_ref[...], preferred_element_type=F32).astype(BF16)
    for c in range(2 * PEER_HEADS):
        q_scr[c] = q[:, c * half:(c + 1) * half]
    rowf = lax.broadcasted_iota(jnp.int32, (N_KEYS, tb), 0).astype(F32)
    nc = 8 * 10
    rowc = lax.broadcasted_iota(jnp.int32, (nc, tb), 0).astype(F32)

    def head_body(h, carry):
        s1 = lax.dot_general(k1_ref[...], q_scr[2 * h], NT_DIMS, preferred_element_type=F32)
        s2 = lax.dot_general(k2_ref[...], q_scr[2 * h + 1], NT_DIMS, preferred_element_type=F32)
        v1, i1 = _top16_rows(s1, rowf, N_KEYS)
        v2, i2 = _top16_rows(s2, rowf, N_KEYS)
        sub8 = lax.broadcasted_iota(jnp.int32, (8, tb), 0)
        v2lo, v2hi = jnp.concatenate(v2[:8], axis=0), jnp.concatenate(v2[8:], axis=0)
        i2lo, i2hi = jnp.concatenate(i2[:8], axis=0), jnp.concatenate(i2[8:], axis=0)
        cand_p = [v1[0] + v2lo, v1[0] + v2hi]
        e1_p = [jnp.broadcast_to(i1[0], (8, tb))] * 2
        e2_p = [i2lo, i2hi]
        for a in range(1, 8):
            nb = PEER_TOPK // (a + 1)
            piece = v1[a] + v2lo
            cand_p.append(piece if nb >= 8 else jnp.where(sub8 < nb, piece, REMOVED))
            e1_p.append(jnp.broadcast_to(i1[a], (8, tb)))
            e2_p.append(i2lo)
        cand_p.append(jnp.concatenate(v1[8:], axis=0) + v2[0])
        e1_p.append(jnp.concatenate(i1[8:], axis=0))
        e2_p.append(jnp.broadcast_to(i2[0], (8, tb)))
        cand = jnp.concatenate(cand_p, axis=0)
        e1c = jnp.concatenate(e1_p, axis=0)
        e2c = jnp.concatenate(e2_p, axis=0)
        vs, e1, e2 = [], [], []
        for _ in range(PEER_TOPK):
            mx = jnp.max(cand, axis=0, keepdims=True)
            idx = jnp.min(jnp.where(cand == mx, rowc, float(nc)), axis=0, keepdims=True)
            pick = rowc == idx
            vs.append(mx)
            e1.append(jnp.sum(jnp.where(pick, e1c, 0.0), axis=0, keepdims=True))
            e2.append(jnp.sum(jnp.where(pick, e2c, 0.0), axis=0, keepdims=True))
            cand = jnp.where(pick, REMOVED, cand)
        vsa = jnp.concatenate(vs, axis=0)
        ex = jnp.exp(vsa - vs[0])
        gate = ex / jnp.sum(ex, axis=0, keepdims=True)
        r0 = pl.multiple_of(h * PEER_TOPK, PEER_TOPK)
        g_scr[pl.ds(r0, PEER_TOPK), :] = gate
        e1_scr[pl.ds(r0, PEER_TOPK), :] = jnp.concatenate(e1, axis=0)
        e2_scr[pl.ds(r0, PEER_TOPK), :] = jnp.concatenate(e2, axis=0)
        return carry

    lax.fori_loop(0, PEER_HEADS, head_body, 0)

    gt_scr[...] = g_scr[...].T
    e1t_scr[...] = e1_scr[...].T
    e2t_scr[...] = e2_scr[...].T
    sub = lax.broadcasted_iota(jnp.int32, (N_KEYS, PEER_HEADS * PEER_TOPK), 0).astype(F32)

    def tok_body(t, carry):
        shape = (N_KEYS, PEER_HEADS * PEER_TOPK)
        g_row = jnp.broadcast_to(gt_scr[pl.ds(t, 1), :], shape)
        e1_row = jnp.broadcast_to(e1t_scr[pl.ds(t, 1), :], shape)
        e2_row = jnp.broadcast_to(e2t_scr[pl.ds(t, 1), :], shape)
        a = jnp.where(e1_row == sub, g_row, 0.0).astype(BF16)
        bm = jnp.where(e2_row == sub, 1.0, 0.0).astype(BF16)
        w_ref[t] = lax.dot_general(a, bm, NT_DIMS, preferred_element_type=F32).astype(w_ref.dtype)
        return carry

    lax.fori_loop(0, tb, tok_body, 0, unroll=8)


def _peer_route(xn, wq, k1, k2):
    t, d = xn.shape
    tb = ROUTE_TB
    half = PEER_DK // 2
    slots = PEER_HEADS * PEER_TOPK
    const = lambda shape: pl.BlockSpec(shape, lambda i: (0,) * len(shape))
    return pl.pallas_call(
        _route_kernel,
        grid=(t // tb,),
        in_specs=[pl.BlockSpec((tb, d), lambda i: (i, 0)), const((d, PEER_HEADS * PEER_DK)),
                  const((N_KEYS, half)), const((N_KEYS, half))],
        out_specs=pl.BlockSpec((tb, N_KEYS, N_KEYS), lambda i: (i, 0, 0)),
        out_shape=jax.ShapeDtypeStruct((t, N_KEYS, N_KEYS), BF16),
        scratch_shapes=[pltpu.VMEM((2 * PEER_HEADS, tb, half), BF16),
                        pltpu.VMEM((slots, tb), F32), pltpu.VMEM((slots, tb), F32), pltpu.VMEM((slots, tb), F32),
                        pltpu.VMEM((tb, slots), F32), pltpu.VMEM((tb, slots), F32), pltpu.VMEM((tb, slots), F32)],
        compiler_params=_cparams(("parallel",)),
        name="peer_route",
    )(xn, wq, k1, k2)


def _castT_kernel(u_ref, o_ref):
    o_ref[...] = u_ref[...].T.astype(o_ref.dtype)


def _cast_transpose(u, te, td):
    ne, d = u.shape
    return pl.pallas_call(
        _castT_kernel,
        grid=(ne // te, d // td),
        in_specs=[pl.BlockSpec((te, td), lambda i, j: (i, j))],
        out_specs=pl.BlockSpec((td, te), lambda i, j: (j, i)),
        out_shape=jax.ShapeDtypeStruct((d, ne), BF16),
        compiler_params=_cparams(("parallel", "parallel")),
        name="expert_u_transpose",
    )(u)


def _dense_kernel(xn_ref, ut_ref, w_ref, v_ref, h_ref, gf_ref, o_ref, acc_ref):
    e = pl.program_id(1)

    @pl.when(e == 0)
    def _():
        acc_ref[...] = jnp.zeros_like(acc_ref)

    a = _gelu(jnp.dot(xn_ref[...], ut_ref[...], preferred_element_type=F32))
    aw = (a * w_ref[...].astype(F32)).astype(BF16)
    acc_ref[...] += jnp.dot(aw, v_ref[...], preferred_element_type=F32)

    @pl.when(e == pl.num_programs(1) - 1)
    def _():
        o_ref[...] = _rms(h_ref[...] + acc_ref[...], gf_ref[...]).astype(o_ref.dtype)


def _peer_dense(xn, ut, wmap, v, h2, norm_final, tb, eb):
    t, d = xn.shape
    ne = ut.shape[1]
    return pl.pallas_call(
        _dense_kernel,
        grid=(t // tb, ne // eb),
        in_specs=[pl.BlockSpec((tb, d), lambda i, e: (i, 0)),
                  pl.BlockSpec((d, eb), lambda i, e: (0, e)),
                  pl.BlockSpec((tb, eb), lambda i, e: (i, e)),
                  pl.BlockSpec((eb, d), lambda i, e: (e, 0)),
                  pl.BlockSpec((tb, d), lambda i, e: (i, 0)),
                  pl.BlockSpec((1, d), lambda i, e: (0, 0))],
        out_specs=pl.BlockSpec((tb, d), lambda i, e: (i, 0)),
        out_shape=jax.ShapeDtypeStruct((t, d), F32),
        scratch_shapes=[pltpu.VMEM((tb, d), F32)],
        compiler_params=_cparams(("parallel", "arbitrary")),
        name="peer_dense",
    )(xn, ut, wmap, v, h2, norm_final.reshape(1, d))


def _alibi_slopes(n):
    return jnp.power(2.0, -8.0 * (jnp.arange(n, dtype=F32) + 1.0) / n)


def _prep_w_in(w_in):
    d = w_in.shape[0]
    wa = w_in[:, :COL_BQ]
    wq = (w_in[:, COL_BQ:COL_BKV] * (B_HD ** -0.5)).reshape(d, B_KV, B_REP, B_HD)
    wq = wq.transpose(0, 2, 1, 3).reshape(d, B_Q_COLS)
    wkv = w_in[:, COL_BKV:COL_BG]
    wg = w_in[:, COL_BG:COL_BG + B_GATE_COLS]
    pad = jnp.zeros((d, COL_M0 - COL_BG - B_GATE_COLS), w_in.dtype)
    wm = w_in[:, COL_BG + B_GATE_COLS:]
    return jnp.concatenate([wa, wq, wkv, wg, pad, wm], axis=1).astype(BF16)


def _block(x, mem, norm_mix, w_in, w_cmp_k1, w_cmp_k2, pe_cmp_k, w_cmp_v1, w_cmp_v2, pe_cmp_v,
           w_up_a, w_up_b, w_out, norm_x, norm_mem, w_xq, w_xkv, w_xo, norm_ffn, w_pq,
           sub_keys1, sub_keys2, expert_u, expert_v, norm_out, out_dtype):
    t, d = x.shape
    assert d == D_MODEL and w_in.shape[1] == COL_BG + B_GATE_COLS + 2 * D_MODEL
    tm = min(1024, t)

    proj = _norm_matmul(x, norm_mix, _prep_w_in(w_in), tm, 512, BF16)

    slopes_a = _alibi_slopes(A_GROUPS * A_HEADS)
    dil = [_dilated_group(proj, slopes_a, g) for g in range(A_GROUPS)]

    kc, vc = _nsa_compress(proj, pe_cmp_k, pe_cmp_v, w_cmp_k1, w_cmp_k2, w_cmp_v1, w_cmp_v2)
    yb = _nsa_attention(proj, kc, vc, _alibi_slopes(B_HEADS))

    wb = w_up_b.reshape(B_KV, B_REP, B_HD, d).transpose(1, 0, 2, 3).reshape(B_HEADS * B_HD, d)
    h1 = _merge(x, proj, [o for o, _ in dil], [l for _, l in dil], yb,
                w_up_a.astype(BF16), wb.astype(BF16), w_out.astype(BF16), min(512, t))

    kvm = _norm_matmul(mem, norm_mem, w_xkv.astype(BF16), mem.shape[0], 512, BF16)
    h2, xn = _cross(h1, norm_x, norm_ffn, w_xq.astype(BF16), kvm, w_xo.astype(BF16), min(512, t))

    wmap = _peer_route(xn, w_pq.astype(BF16), sub_keys1.astype(BF16), sub_keys2.astype(BF16))
    wmap = wmap.reshape(t, N_EXPERTS)
    return _peer_dense(xn, _cast_transpose(expert_u, 512, 1024), wmap, expert_v.astype(BF16), h2, norm_out,
                       min(512, t), 1024).astype(out_dtype)


@jax.jit
def kernel(x, mem, norm_mix, w_in, w_cmp_k1, w_cmp_k2, pe_cmp_k, w_cmp_v1, w_cmp_v2, pe_cmp_v, w_up_a, w_up_b,
           w_out, norm_x, norm_mem, w_xq, w_xkv, w_xo, norm_ffn, w_pq, sub_keys1, sub_keys2, expert_u,
           expert_v, norm_final):
    assert x.shape[0] == 1 and mem.shape[0] == 1 and norm_mix.shape[0] == 1
    out = _block(x[0], mem[0], norm_mix[0], w_in[0], w_cmp_k1[0], w_cmp_k2[0], pe_cmp_k[0], w_cmp_v1[0],
                 w_cmp_v2[0], pe_cmp_v[0], w_up_a[0], w_up_b[0], w_out[0], norm_x[0], norm_mem[0], w_xq[0],
                 w_xkv[0], w_xo[0], norm_ffn[0], w_pq[0], sub_keys1[0], sub_keys2[0], expert_u[0],
                 expert_v[0], norm_final, x.dtype)
    return out[None]
```

```python
import functools

import jax
import jax.numpy as jnp
from jax import lax
from jax.experimental import pallas as pl
from jax.experimental.pallas import tpu as pltpu

F32 = jnp.float32
BF16 = jnp.bfloat16

D_MODEL = 2048
BLK = 128
EPS = 1e-6
NEG = -1e30
TINY = 1e-30
FORCE = 1e4
REMOVED = -3e38

A_DILATIONS = (1, 4, 16)
A_WINDOWS = (128, 512, 2048)
A_GROUPS = 3
A_HEADS = 4
A_HD = 128
B_HEADS = 16
B_KV = 2
B_REP = B_HEADS // B_KV
B_HD = 64
CMP_LEN = 32
CMP_STRIDE = 16
CMP_HIDDEN = 128
SLC_LEN = 64
SLC_SHIFT = 6
N_SEL = 16
WIN = 512
X_HEADS = 4
X_HD = 128
PEER_HEADS = 8
PEER_DK = 256
N_KEYS = 128
N_EXPERTS = N_KEYS * N_KEYS
PEER_TOPK = 16

A_QKV_COLS = 3 * A_GROUPS * A_HEADS * A_HD
A_OUT = A_HEADS * A_HD
B_Q_COLS = B_HEADS * B_HD
B_KV_COLS = 3 * 2 * B_KV * B_HD
B_GATE_COLS = B_HEADS * 3

COL_A = 0
COL_BQ = COL_A + A_QKV_COLS
COL_BKV = COL_BQ + B_Q_COLS
COL_BG = COL_BKV + B_KV_COLS
COL_M0 = 6656
COL_M1 = COL_M0 + D_MODEL
PROJ_COLS = COL_M1 + D_MODEL

VMEM_LIMIT = 56 * 1024 * 1024

NT_DIMS = (((1,), (1,)), ((), ()))


def _cparams(sem):
    return pltpu.CompilerParams(dimension_semantics=sem, vmem_limit_bytes=VMEM_LIMIT)


def _rms(x, g):
    ms = jnp.mean(x * x, axis=-1, keepdims=True)
    return x * lax.rsqrt(ms + EPS) * g


def _gelu(x):
    return 0.5 * x * (1.0 + jnp.tanh(0.7978845608028654 * (x + 0.044715 * (x * x * x))))


def _sigmoid(x):
    return 1.0 / (1.0 + jnp.exp(-x))


def _nmm_kernel(x_ref, g_ref, w_ref, o_ref, xn_ref):
    @pl.when(pl.program_id(1) == 0)
    def _():
        xn_ref[...] = _rms(x_ref[...].astype(F32), g_ref[...]).astype(BF16)

    o_ref[...] = jnp.dot(xn_ref[...], w_ref[...], preferred_element_type=F32).astype(o_ref.dtype)


def _norm_matmul(x, gain, w, tm, tn, out_dtype):
    t, d = x.shape
    n = w.shape[1]
    return pl.pallas_call(
        _nmm_kernel,
        grid=(t // tm, n // tn),
        in_specs=[pl.BlockSpec((tm, d), lambda i, j: (i, 0)),
                  pl.BlockSpec((1, d), lambda i, j: (0, 0)),
                  pl.BlockSpec((d, tn), lambda i, j: (0, j))],
        out_specs=pl.BlockSpec((tm, tn), lambda i, j: (i, j)),
        out_shape=jax.ShapeDtypeStruct((t, n), out_dtype),
        scratch_shapes=[pltpu.VMEM((tm, d), BF16)],
        compiler_params=_cparams(("parallel", "arbitrary")),
        name="norm_matmul",
    )(x, gain.reshape(1, d), w)


def _dil_kernel(slopes_ref, q_ref, kp_ref, ko_ref, vp_ref, vo_ref, o_ref, lse_ref, *, dil, qb, group):
    b = pl.program_id(1)
    nk = qb + BLK
    q = q_ref[...]
    k = jnp.concatenate([kp_ref[...], ko_ref[...]], axis=0)
    v = jnp.concatenate([vp_ref[...], vo_ref[...]], axis=0)
    i = lax.broadcasted_iota(jnp.int32, (qb, nk), 0)
    j = lax.broadcasted_iota(jnp.int32, (qb, nk), 1)
    rel = i + BLK - j
    jmin = jnp.where(b > 0, 0, BLK)
    ok = (rel >= 0) & (rel <= BLK) & (j >= jmin)
    dist = (rel * dil).astype(F32)
    scale = A_HD ** -0.5
    for h in range(A_HEADS):
        sl = slice(h * A_HD, (h + 1) * A_HD)
        s = lax.dot_general(q[:, sl], k[:, sl], NT_DIMS, preferred_element_type=F32) * scale
        s = s - slopes_ref[group * A_HEADS + h] * dist
        s = jnp.where(ok, s, NEG)
        m = jnp.max(s, axis=-1, keepdims=True)
        p = jnp.exp(s - m)
        den = jnp.sum(p, axis=-1, keepdims=True)
        o = jnp.dot(p.astype(BF16), v[:, sl], preferred_element_type=F32) / den
        o_ref[:, sl] = o
        lse_ref[:, sl] = jnp.broadcast_to(m + jnp.log(den), (qb, A_HD))


def _dilated_group(proj, slopes, group):
    t = proj.shape[0]
    dil = A_DILATIONS[group]
    assert A_WINDOWS[group] // dil == BLK
    l = t // dil
    qb = min(256, l)
    assert l % qb == 0 and qb % BLK == 0
    if dil == 1:
        src = proj
        stride = PROJ_COLS // A_OUT
    else:
        src = proj[:, :A_QKV_COLS].reshape(l, dil * A_QKV_COLS)
        stride = A_QKV_COLS // A_OUT
    ngh = A_GROUPS

    def col(s):
        return lambda r, b: (b, r * stride + s * ngh + group)

    def colp(s):
        return lambda r, b: (jnp.maximum(b * (qb // BLK) - 1, 0), r * stride + s * ngh + group)

    o, lse = pl.pallas_call(
        functools.partial(_dil_kernel, dil=dil, qb=qb, group=group),
        grid=(dil, l // qb),
        in_specs=[pl.BlockSpec(memory_space=pltpu.SMEM),
                  pl.BlockSpec((qb, A_OUT), col(0)),
                  pl.BlockSpec((BLK, A_OUT), colp(1)),
                  pl.BlockSpec((qb, A_OUT), col(1)),
                  pl.BlockSpec((BLK, A_OUT), colp(2)),
                  pl.BlockSpec((qb, A_OUT), col(2))],
        out_specs=[pl.BlockSpec((qb, A_OUT), lambda r, b: (b, r)),
                   pl.BlockSpec((qb, A_OUT), lambda r, b: (b, r))],
        out_shape=[jax.ShapeDtypeStruct((l, dil * A_OUT), F32),
                   jax.ShapeDtypeStruct((l, dil * A_OUT), F32)],
        compiler_params=_cparams(("parallel", "parallel")),
        name=f"dilated_attn_g{group}",
    )(slopes, src, src, src, src, src)
    return o.reshape(t, A_OUT), lse.reshape(t, A_OUT)


def _cmp_kernel(a_ref, pe_ref, w1_ref, w2_ref, o_ref):
    a = a_ref[0]
    n16 = a.shape[0]
    nxt = pltpu.roll(a, n16 - 1, axis=0)
    blocks = jnp.concatenate([a, nxt], axis=1) + pe_ref[0]
    hid = _gelu(jnp.dot(blocks.astype(BF16), w1_ref[0], preferred_element_type=F32))
    out = jnp.dot(hid.astype(BF16), w2_ref[0], preferred_element_type=F32)
    row = lax.broadcasted_iota(jnp.int32, out.shape, 0)
    o_ref[0] = jnp.where(row < n16 - 1, out, 0.0)


def _nsa_compress(proj, pe_k, pe_v, w_k1, w_k2, w_v1, w_v2):
    t = proj.shape[0]
    assert CMP_LEN == 2 * CMP_STRIDE and t % CMP_STRIDE == 0
    n16 = t // CMP_STRIDE
    kv = proj[:, COL_BKV:COL_BKV + 2 * B_KV * B_HD].astype(F32)
    a4 = kv.reshape(t, 2 * B_KV, B_HD).transpose(1, 0, 2).reshape(2 * B_KV, n16, CMP_STRIDE * B_HD)
    pe = jnp.stack([pe_k.reshape(1, -1), pe_v.reshape(1, -1)])
    w1 = jnp.stack([w_k1, w_v1]).astype(BF16)
    w2 = jnp.stack([w_k2, w_v2]).astype(BF16)
    kd = CMP_LEN * B_HD
    out = pl.pallas_call(
        _cmp_kernel,
        grid=(2 * B_KV,),
        in_specs=[pl.BlockSpec((1, n16, CMP_STRIDE * B_HD), lambda i: (i, 0, 0)),
                  pl.BlockSpec((1, 1, kd), lambda i: (i // B_KV, 0, 0)),
                  pl.BlockSpec((1, kd, CMP_HIDDEN), lambda i: (i // B_KV, 0, 0)),
                  pl.BlockSpec((1, CMP_HIDDEN, B_HD), lambda i: (i // B_KV, 0, 0))],
        out_specs=pl.BlockSpec((1, n16, B_HD), lambda i: (i, 0, 0)),
        out_shape=jax.ShapeDtypeStruct((2 * B_KV, n16, B_HD), F32),
        compiler_params=_cparams(("parallel",)),
        name="nsa_compress",
    )(a4, pe, w1, w2)
    kc = jnp.concatenate([out[0], out[1]], axis=-1).astype(BF16)
    vc = jnp.concatenate([out[2], out[3]], axis=-1).astype(BF16)
    return kc, vc


SEL_KT = 512
SEL_KT_SHIFT = 9
WIN_KEYS = WIN + BLK


def _lanes(col):
    return jnp.broadcast_to(col, (col.shape[0], 2 * B_HD))


def _tile_lanes(rep, n):
    return jnp.concatenate([rep] * (n // (2 * B_HD)), axis=1)


def _nsa_kernel(slopes_ref, q0_ref, q1_ref, gate_ref, kc_ref, vc_ref, ks_ref, vs_ref, kw_ref, vw_ref,
                ovt_ref, ex_ref, o_ref, qg_ref, oc_ref, ow_ref, acc_ref, m_ref, a_ref, z_ref, p_ref, ps_ref):
    b = pl.program_id(0)
    t0 = b * BLK
    rows = B_REP * BLK
    ncmp = kc_ref.shape[0]
    nslc = ovt_ref.shape[0]
    ntile_all = ex_ref.shape[0]
    lane = lax.broadcasted_iota(jnp.int32, (1, 2 * B_HD), 1)
    lo = lane < B_HD
    tq = t0 + lax.broadcasted_iota(jnp.int32, (BLK, 1), 0)
    tq_lane = t0 + lax.broadcasted_iota(jnp.int32, (1, BLK), 1)
    qcat = jnp.concatenate([q0_ref[...], q1_ref[...]], axis=1)
    gates = _sigmoid(gate_ref[...].astype(F32))
    zero = jnp.zeros((), BF16)
    one = jnp.ones((), BF16)

    cend = lax.broadcasted_iota(jnp.int32, (1, ncmp), 1) * CMP_STRIDE + (CMP_LEN - 1)
    ok_c = cend <= tq
    dist_c = (t0 + BLK - 1 - cend).astype(F32)
    halves = [lo, jnp.logical_not(lo)]
    slopes = [[slopes_ref[g * B_REP + i] for i in range(B_REP)] for g in range(B_KV)]

    for g in range(B_KV):
        slope = slopes[g]
        for i in range(B_REP):
            qg_ref[g, i * BLK:(i + 1) * BLK, :] = jnp.where(halves[g], qcat[:, i * 2 * B_HD:(i + 1) * 2 * B_HD], zero)
        s = lax.dot_general(qg_ref[g], kc_ref[...], NT_DIMS, preferred_element_type=F32)
        for i in range(B_REP):
            rs = slice(i * BLK, (i + 1) * BLK)
            z = jnp.where(ok_c, s[rs] - slope[i] * dist_c, NEG)
            z_ref[rs, :ncmp] = z
            m_ref[rs, :] = _lanes(jnp.max(z, axis=-1, keepdims=True))

        @pl.when(b >= 0)
        def _():
            for i in range(B_REP):
                rs = slice(i * BLK, (i + 1) * BLK)
                e = jnp.where(ok_c, jnp.exp(z_ref[rs, :ncmp] - _tile_lanes(m_ref[rs, :], ncmp)), 0.0)
                z_ref[rs, :ncmp] = e
                a_ref[rs, :] = _lanes(jnp.maximum(jnp.sum(e, axis=-1, keepdims=True), TINY))

        @pl.when(b >= 0)
        def _():
            psum = jnp.zeros((BLK, ncmp), F32)
            for i in range(B_REP):
                rs = slice(i * BLK, (i + 1) * BLK)
                p = z_ref[rs, :ncmp] / _tile_lanes(a_ref[rs, :], ncmp)
                psum = psum + p
                p_ref[rs, :ncmp] = p.astype(BF16)
            oc_ref[g] = jnp.dot(p_ref[:, :ncmp], vc_ref[...], preferred_element_type=F32)
            ps_ref[g] = psum

    imps = []
    for g in range(B_KV):
        psum = ps_ref[g]
        p_hi = psum.astype(BF16)
        p_lo = (psum - p_hi.astype(F32)).astype(BF16)
        imps.append(lax.dot_general(ovt_ref[...], p_hi, NT_DIMS, preferred_element_type=F32)
                    + lax.dot_general(ovt_ref[...], p_lo, NT_DIMS, preferred_element_type=F32))
    imp = jnp.concatenate(imps, axis=1)
    jj = lax.broadcasted_iota(jnp.int32, (nslc, B_KV * BLK), 0)
    jf = jj.astype(F32)
    cur = jnp.right_shift(jnp.concatenate([tq_lane] * B_KV, axis=1), SLC_SHIFT)
    forced = (jj == 0) | (jj == cur) | (jj == cur - 1)
    score = jnp.where(forced, FORCE, jnp.where(jj <= cur, imp, NEG))
    sel_t = jnp.zeros((nslc, B_KV * BLK), F32)
    for _ in range(min(N_SEL, nslc)):
        mx = jnp.max(score, axis=0, keepdims=True)
        idx = jnp.min(jnp.where(score == mx, jf, 1e9), axis=0, keepdims=True)
        pick = jf == idx
        sel_t = jnp.where(pick & (mx > NEG / 2), 1.0, sel_t)
        score = jnp.where(pick, REMOVED, score)
    ntp = max(8, ntile_all)
    tile_of = jnp.right_shift(lax.broadcasted_iota(jnp.int32, (ntp, nslc), 1), SEL_KT_SHIFT - SLC_SHIFT)
    krow = lax.broadcasted_iota(jnp.int32, (ntp, nslc), 0)
    gmat = jnp.where(tile_of == krow, 1.0, 0.0).astype(BF16)
    cnt = jnp.dot(gmat, sel_t.astype(BF16), preferred_element_type=F32)
    pow2 = jnp.left_shift(1, lax.broadcasted_iota(jnp.int32, (ntp, 1), 0)).astype(F32)
    sel_bs, bitss = [], []
    for g in range(B_KV):
        gl = slice(g * BLK, (g + 1) * BLK)
        sel_bs.append(sel_t[:, gl].T.astype(BF16))
        anyq = jnp.max(cnt[:, gl], axis=-1, keepdims=True)
        bitss.append(jnp.sum(jnp.where(anyq > 0.5, pow2, 0.0)).astype(jnp.int32))

    for g in range(B_KV):
        half, slope, sel_b, bits = halves[g], slopes[g], sel_bs[g], bitss[g]

        wstart = pl.multiple_of(jnp.maximum(t0 - WIN, 0), BLK)
        uw = wstart + lax.broadcasted_iota(jnp.int32, (1, WIN_KEYS), 1)
        dw = tq - uw
        ok_w = (dw >= 0) & (dw < WIN)
        dist_w = (t0 + BLK - 1 - uw).astype(F32)
        s = lax.dot_general(qg_ref[g], kw_ref[pl.ds(wstart, WIN_KEYS), :], NT_DIMS, preferred_element_type=F32)
        for i in range(B_REP):
            rs = slice(i * BLK, (i + 1) * BLK)
            z = jnp.where(ok_w, s[rs] - slope[i] * dist_w, NEG)
            z_ref[rs, :] = z
            m_ref[rs, :] = _lanes(jnp.max(z, axis=-1, keepdims=True))

        @pl.when(b >= 0)
        def _():
            for i in range(B_REP):
                rs = slice(i * BLK, (i + 1) * BLK)
                p_ref[rs, :] = jnp.exp(z_ref[rs, :] - _tile_lanes(m_ref[rs, :], WIN_KEYS)).astype(BF16)
            vw = jnp.where(half, vw_ref[pl.ds(wstart, WIN_KEYS), :], one)
            ow = jnp.dot(p_ref[...], vw, preferred_element_type=F32)
            ow_ref[...] = ow / pltpu.roll(ow, B_HD, axis=1)

        m_ref[...] = jnp.full((rows, 2 * B_HD), NEG, F32)
        acc_ref[...] = jnp.zeros((rows, 2 * B_HD), F32)
        ntiles = (t0 + BLK + SEL_KT - 1) // SEL_KT

        def tile_body(it, carry):
            kt = ntiles - 1 - it

            @pl.when(jnp.bitwise_and(jnp.right_shift(bits, kt), 1) == 1)
            def _():
                k0 = pl.multiple_of(kt * SEL_KT, SEL_KT)
                us = k0 + lax.broadcasted_iota(jnp.int32, (1, SEL_KT), 1)
                selx = jnp.dot(sel_b, ex_ref[kt], preferred_element_type=F32)
                ok_s = (selx > 0.5) & (us <= tq)
                dist_s = (t0 + BLK - 1 - us).astype(F32)
                s = lax.dot_general(qg_ref[g], ks_ref[pl.ds(k0, SEL_KT), :], NT_DIMS, preferred_element_type=F32)
                for i in range(B_REP):
                    rs = slice(i * BLK, (i + 1) * BLK)
                    z = jnp.where(ok_s, s[rs] - slope[i] * dist_s, NEG)
                    z_ref[rs, :SEL_KT] = z
                    m_old = m_ref[rs, :]
                    m_new = jnp.maximum(m_old, _lanes(jnp.max(z, axis=-1, keepdims=True)))
                    a_ref[rs, :] = jnp.exp(m_old - m_new)
                    m_ref[rs, :] = m_new

                @pl.when(kt >= 0)
                def _():
                    for i in range(B_REP):
                        rs = slice(i * BLK, (i + 1) * BLK)
                        p_ref[rs, :SEL_KT] = jnp.exp(z_ref[rs, :SEL_KT]
                                                     - _tile_lanes(m_ref[rs, :], SEL_KT)).astype(BF16)
                    vs = jnp.where(half, vs_ref[pl.ds(k0, SEL_KT), :], one)
                    acc_ref[...] = a_ref[...] * acc_ref[...] + jnp.dot(p_ref[:, :SEL_KT], vs,
                                                                       preferred_element_type=F32)

            return carry

        lax.fori_loop(0, ntiles, tile_body, 0)

        for i in range(B_REP):
            rs = slice(i * BLK, (i + 1) * BLK)
            h = g * B_REP + i
            acc = acc_ref[rs, :]
            o_s = acc / pltpu.roll(acc, B_HD, axis=1)
            out = (gates[:, 3 * h:3 * h + 1] * oc_ref[g, rs, :] + gates[:, 3 * h + 1:3 * h + 2] * o_s
                   + gates[:, 3 * h + 2:3 * h + 3] * ow_ref[rs, :])
            cs = slice(i * 2 * B_HD, (i + 1) * 2 * B_HD)
            if g == 0:
                o_ref[:, cs] = out.astype(o_ref.dtype)
            else:
                o_ref[:, cs] = jnp.where(lo, o_ref[:, cs], out.astype(o_ref.dtype))


def _nsa_attention(proj, kc, vc, slopes):
    t = proj.shape[0]
    assert t % SEL_KT == 0 and SLC_LEN == 1 << SLC_SHIFT and SEL_KT == 1 << SEL_KT_SHIFT
    ncmp = kc.shape[0]
    nslc = t // SLC_LEN
    ntile = t // SEL_KT
    assert ntile <= 24
    c_start = jnp.arange(ncmp)[None, :] * CMP_STRIDE
    s_start = jnp.arange(nslc)[:, None] * SLC_LEN
    overlap = jnp.clip(jnp.minimum(c_start + CMP_LEN, s_start + SLC_LEN) - jnp.maximum(c_start, s_start),
                       0, None).astype(F32) / CMP_LEN
    overlap = jnp.where(jnp.arange(ncmp)[None, :] < ncmp - 1, overlap, 0.0).astype(BF16)
    key_blk = (jnp.arange(ntile)[:, None, None] * SEL_KT + jnp.arange(SEL_KT)[None, None, :]) // SLC_LEN
    expand = (key_blk == jnp.arange(nslc)[None, :, None]).astype(BF16)
    w2 = 2 * B_HD
    qcol = COL_BQ // (4 * w2)
    kvcol = COL_BKV // w2
    full = lambda c: pl.BlockSpec((t, w2), lambda b: (0, c))
    rows = B_REP * BLK
    return pl.pallas_call(
        _nsa_kernel,
        grid=(t // BLK,),
        in_specs=[pl.BlockSpec(memory_space=pltpu.SMEM),
                  pl.BlockSpec((BLK, 4 * w2), lambda b: (b, qcol)),
                  pl.BlockSpec((BLK, 4 * w2), lambda b: (b, qcol + 1)),
                  pl.BlockSpec((BLK, w2), lambda b: (b, COL_BG // w2)),
                  pl.BlockSpec((ncmp, w2), lambda b: (0, 0)),
                  pl.BlockSpec((ncmp, w2), lambda b: (0, 0)),
                  full(kvcol + 2), full(kvcol + 3), full(kvcol + 4), full(kvcol + 5),
                  pl.BlockSpec((nslc, ncmp), lambda b: (0, 0)),
                  pl.BlockSpec((ntile, nslc, SEL_KT), lambda b: (0, 0, 0))],
        out_specs=pl.BlockSpec((BLK, B_HEADS * B_HD), lambda b: (b, 0)),
        out_shape=jax.ShapeDtypeStruct((t, B_HEADS * B_HD), BF16),
        scratch_shapes=[pltpu.VMEM((B_KV, rows, w2), BF16),
                        pltpu.VMEM((B_KV, rows, w2), F32),
                        pltpu.VMEM((rows, w2), F32),
                        pltpu.VMEM((rows, w2), F32),
                        pltpu.VMEM((rows, w2), F32),
                        pltpu.VMEM((rows, w2), F32),
                        pltpu.VMEM((rows, WIN_KEYS), F32),
                        pltpu.VMEM((rows, WIN_KEYS), BF16),
                        pltpu.VMEM((B_KV, BLK, ncmp), F32)],
        compiler_params=_cparams(("parallel",)),
        name="nsa_attention",
    )(slopes, proj, proj, proj, kc, vc, proj, proj, proj, proj, overlap, expand)


MERGE_KC = 512


def _merge_kernel(o0_ref, o1_ref, o2_ref, l0_ref, l1_ref, l2_ref, yb_ref, m0_ref, m1_ref, x_ref,
                  wa_ref, wb_ref, wo_ref, h_ref, ya_ref):
    kc = pl.program_id(1)

    @pl.when(kc == 0)
    def _():
        l0, l1, l2 = l0_ref[...], l1_ref[...], l2_ref[...]
        m = jnp.maximum(jnp.maximum(l0, l1), l2)
        e0, e1, e2 = jnp.exp(l0 - m), jnp.exp(l1 - m), jnp.exp(l2 - m)
        den = e0 + e1 + e2
        ya = (e0 / den) * o0_ref[...] + (e1 / den) * o1_ref[...] + (e2 / den) * o2_ref[...]
        ya_ref[...] = ya.astype(BF16)
        h_ref[...] = x_ref[...]

    ua = jnp.dot(ya_ref[...], wa_ref[...], preferred_element_type=F32)
    ub = jnp.dot(yb_ref[...], wb_ref[...], preferred_element_type=F32)
    mix = _sigmoid(m0_ref[...].astype(F32)) * ua + _sigmoid(m1_ref[...].astype(F32)) * ub
    h_ref[...] += jnp.dot(mix.astype(BF16), wo_ref[...], preferred_element_type=F32)


def _merge(x, proj, dil_o, dil_lse, yb, wa, wb, wo, tm):
    t, d = x.shape
    nk = d // MERGE_KC
    row = lambda w: pl.BlockSpec((tm, w), lambda i, k: (i, 0))
    return pl.pallas_call(
        _merge_kernel,
        grid=(t // tm, nk),
        in_specs=[row(A_OUT)] * 6 + [row(B_HEADS * B_HD),
                  pl.BlockSpec((tm, MERGE_KC), lambda i, k: (i, COL_M0 // MERGE_KC + k)),
                  pl.BlockSpec((tm, MERGE_KC), lambda i, k: (i, COL_M1 // MERGE_KC + k)),
                  row(d),
                  pl.BlockSpec((A_OUT, MERGE_KC), lambda i, k: (0, k)),
                  pl.BlockSpec((B_HEADS * B_HD, MERGE_KC), lambda i, k: (0, k)),
                  pl.BlockSpec((MERGE_KC, d), lambda i, k: (k, 0))],
        out_specs=pl.BlockSpec((tm, d), lambda i, k: (i, 0)),
        out_shape=jax.ShapeDtypeStruct((t, d), F32),
        scratch_shapes=[pltpu.VMEM((tm, A_OUT), BF16)],
        compiler_params=_cparams(("parallel", "arbitrary")),
        name="merge",
    )(*dil_o, *dil_lse, yb, proj, proj, x, wa, wb, wo)


def _cross_kernel(h_ref, gx_ref, gf_ref, wq_ref, kv_ref, wo_ref, h2_ref, xn_ref):
    h = h_ref[...]
    hn = _rms(h, gx_ref[...]).astype(BF16)
    q = jnp.dot(hn, wq_ref[...], preferred_element_type=F32).astype(BF16)
    kv = kv_ref[...]
    nh = X_HEADS * X_HD
    outs = []
    for hd in range(X_HEADS):
        sl = slice(hd * X_HD, (hd + 1) * X_HD)
        s = lax.dot_general(q[:, sl], kv[:, sl], NT_DIMS, preferred_element_type=F32) * (X_HD ** -0.5)
        m = jnp.max(s, axis=-1, keepdims=True)
        p = jnp.exp(s - m)
        p = p / jnp.sum(p, axis=-1, keepdims=True)
        outs.append(jnp.dot(p.astype(BF16), kv[:, nh + hd * X_HD:nh + (hd + 1) * X_HD],
                            preferred_element_type=F32))
    o = jnp.concatenate(outs, axis=1).astype(BF16)
    h2 = h + jnp.dot(o, wo_ref[...], preferred_element_type=F32)
    h2_ref[...] = h2
    xn_ref[...] = _rms(h2, gf_ref[...]).astype(BF16)


def _cross(h1, norm_x, norm_ffn, wq, kv, wo, tm):
    t, d = h1.shape
    nm = kv.shape[0]
    nh = X_HEADS * X_HD
    const = lambda shape: pl.BlockSpec(shape, lambda i: (0, 0))
    return pl.pallas_call(
        _cross_kernel,
        grid=(t // tm,),
        in_specs=[pl.BlockSpec((tm, d), lambda i: (i, 0)), const((1, d)), const((1, d)),
                  const((d, nh)), const((nm, 2 * nh)), const((nh, d))],
        out_specs=[pl.BlockSpec((tm, d), lambda i: (i, 0)), pl.BlockSpec((tm, d), lambda i: (i, 0))],
        out_shape=[jax.ShapeDtypeStruct((t, d), F32), jax.ShapeDtypeStruct((t, d), BF16)],
        compiler_params=_cparams(("parallel",)),
        name="cross_attn",
    )(h1, norm_x.reshape(1, d), norm_ffn.reshape(1, d), wq, kv, wo)


ROUTE_TB = 128


def _top16_rows(s, rowf, nrows):
    vals, idxs = [], []
    for _ in range(PEER_TOPK):
        mx = jnp.max(s, axis=0, keepdims=True)
        idx = jnp.min(jnp.where(s == mx, rowf, float(nrows)), axis=0, keepdims=True)
        vals.append(mx)
        idxs.append(idx)
        s = jnp.where(rowf == idx, REMOVED, s)
    return vals, idxs


def _route_kernel(xn_ref, wq_ref, k1_ref, k2_ref, w_ref, q_scr, g_scr, e1_scr, e2_scr, gt_scr, e1t_scr, e2t_scr):
    tb = xn_ref.shape[0]
    half = PEER_DK // 2
    q = jnp.dot(xn_ref[...], wq_ref[...], preferred_element_type=F32).astype(BF16)
    for c in range(2 * PEER_HEADS):
        q_scr[c] = q[:, c * half:(c + 1) * half]
    rowf = lax.broadcasted_iota(jnp.int32, (N_KEYS, tb), 0).astype(F32)
    nc = 8 * 10
    rowc = lax.broadcasted_iota(jnp.int32, (nc, tb), 0).astype(F32)

    def head_body(h, carry):
        s1 = lax.dot_general(k1_ref[...], q_scr[2 * h], NT_DIMS, preferred_element_type=F32)
        s2 = lax.dot_general(k2_ref[...], q_scr[2 * h + 1], NT_DIMS, preferred_element_type=F32)
        v1, i1 = _top16_rows(s1, rowf, N_KEYS)
        v2, i2 = _top16_rows(s2, rowf, N_KEYS)
        sub8 = lax.broadcasted_iota(jnp.int32, (8, tb), 0)
        v2lo, v2hi = jnp.concatenate(v2[:8], axis=0), jnp.concatenate(v2[8:], axis=0)
        i2lo, i2hi = jnp.concatenate(i2[:8], axis=0), jnp.concatenate(i2[8:], axis=0)
        cand_p = [v1[0] + v2lo, v1[0] + v2hi]
        e1_p = [jnp.broadcast_to(i1[0], (8, tb))] * 2
        e2_p = [i2lo, i2hi]
        for a in range(1, 8):
            nb = PEER_TOPK // (a + 1)
            piece = v1[a] + v2lo
            cand_p.append(piece if nb >= 8 else jnp.where(sub8 < nb, piece, REMOVED))
            e1_p.append(jnp.broadcast_to(i1[a], (8, tb)))
            e2_p.append(i2lo)
        cand_p.append(jnp.concatenate(v1[8:], axis=0) + v2[0])
        e1_p.append(jnp.concatenate(i1[8:], axis=0))
        e2_p.append(jnp.broadcast_to(i2[0], (8, tb)))
        cand = jnp.concatenate(cand_p, axis=0)
        e1c = jnp.concatenate(e1_p, axis=0)
        e2c = jnp.concatenate(e2_p, axis=0)
        vs, e1, e2 = [], [], []
        for _ in range(PEER_TOPK):
            mx = jnp.max(cand, axis=0, keepdims=True)
            idx = jnp.min(jnp.where(cand == mx, rowc, float(nc)), axis=0, keepdims=True)
            pick = rowc == idx
            vs.append(mx)
            e1.append(jnp.sum(jnp.where(pick, e1c, 0.0), axis=0, keepdims=True))
            e2.append(jnp.sum(jnp.where(pick, e2c, 0.0), axis=0, keepdims=True))
            cand = jnp.where(pick, REMOVED, cand)
        vsa = jnp.concatenate(vs, axis=0)
        ex = jnp.exp(vsa - vs[0])
        gate = ex / jnp.sum(ex, axis=0, keepdims=True)
        r0 = pl.multiple_of(h * PEER_TOPK, PEER_TOPK)
        g_scr[pl.ds(r0, PEER_TOPK), :] = gate
        e1_scr[pl.ds(r0, PEER_TOPK), :] = jnp.concatenate(e1, axis=0)
        e2_scr[pl.ds(r0, PEER_TOPK), :] = jnp.concatenate(e2, axis=0)
        return carry

    lax.fori_loop(0, PEER_HEADS, head_body, 0)

    gt_scr[...] = g_scr[...].T
    e1t_scr[...] = e1_scr[...].T
    e2t_scr[...] = e2_scr[...].T
    sub = lax.broadcasted_iota(jnp.int32, (N_KEYS, PEER_HEADS * PEER_TOPK), 0).astype(F32)

    def tok_body(t, carry):
        shape = (N_KEYS, PEER_HEADS * PEER_TOPK)
        g_row = jnp.broadcast_to(gt_scr[pl.ds(t, 1), :], shape)
        e1_row = jnp.broadcast_to(e1t_scr[pl.ds(t, 1), :], shape)
        e2_row = jnp.broadcast_to(e2t_scr[pl.ds(t, 1), :], shape)
        a = jnp.where(e1_row == sub, g_row, 0.0).astype(BF16)
        bm = jnp.where(e2_row == sub, 1.0, 0.0).astype(BF16)
        w_ref[t] = lax.dot_general(a, bm, NT_DIMS, preferred_element_type=F32).astype(w_ref.dtype)
        return carry

    lax.fori_loop(0, tb, tok_body, 0, unroll=8)


def _peer_route(xn, wq, k1, k2):
    t, d = xn.shape
    tb = ROUTE_TB
    half = PEER_DK // 2
    slots = PEER_HEADS * PEER_TOPK
    const = lambda shape: pl.BlockSpec(shape, lambda i: (0,) * len(shape))
    return pl.pallas_call(
        _route_kernel,
        grid=(t // tb,),
        in_specs=[pl.BlockSpec((tb, d), lambda i: (i, 0)), const((d, PEER_HEADS * PEER_DK)),
                  const((N_KEYS, half)), const((N_KEYS, half))],
        out_specs=pl.BlockSpec((tb, N_KEYS, N_KEYS), lambda i: (i, 0, 0)),
        out_shape=jax.ShapeDtypeStruct((t, N_KEYS, N_KEYS), BF16),
        scratch_shapes=[pltpu.VMEM((2 * PEER_HEADS, tb, half), BF16),
                        pltpu.VMEM((slots, tb), F32), pltpu.VMEM((slots, tb), F32), pltpu.VMEM((slots, tb), F32),
                        pltpu.VMEM((tb, slots), F32), pltpu.VMEM((tb, slots), F32), pltpu.VMEM((tb, slots), F32)],
        compiler_params=_cparams(("parallel",)),
        name="peer_route",
    )(xn, wq, k1, k2)


def _castT_kernel(u_ref, o_ref):
    o_ref[...] = u_ref[...].T.astype(o_ref.dtype)


def _cast_transpose(u, te, td):
    ne, d = u.shape
    return pl.pallas_call(
        _castT_kernel,
        grid=(ne // te, d // td),
        in_specs=[pl.BlockSpec((te, td), lambda i, j: (i, j))],
        out_specs=pl.BlockSpec((td, te), lambda i, j: (j, i)),
        out_shape=jax.ShapeDtypeStruct((d, ne), BF16),
        compiler_params=_cparams(("parallel", "parallel")),
        name="expert_u_transpose",
    )(u)


def _dense_kernel(xn_ref, ut_ref, w_ref, v_ref, h_ref, gf_ref, o_ref, acc_ref):
    e = pl.program_id(1)

    @pl.when(e == 0)
    def _():
        acc_ref[...] = jnp.zeros_like(acc_ref)

    a = _gelu(jnp.dot(xn_ref[...], ut_ref[...], preferred_element_type=F32))
    aw = (a * w_ref[...].astype(F32)).astype(BF16)
    acc_ref[...] += jnp.dot(aw, v_ref[...], preferred_element_type=F32)

    @pl.when(e == pl.num_programs(1) - 1)
    def _():
        o_ref[...] = _rms(h_ref[...] + acc_ref[...], gf_ref[...]).astype(o_ref.dtype)


def _peer_dense(xn, ut, wmap, v, h2, norm_final, tb, eb):
    t, d = xn.shape
    ne = ut.shape[1]
    return pl.pallas_call(
        _dense_kernel,
        grid=(t // tb, ne // eb),
        in_specs=[pl.BlockSpec((tb, d), lambda i, e: (i, 0)),
                  pl.BlockSpec((d, eb), lambda i, e: (0, e)),
                  pl.BlockSpec((tb, eb), lambda i, e: (i, e)),
                  pl.BlockSpec((eb, d), lambda i, e: (e, 0)),
                  pl.BlockSpec((tb, d), lambda i, e: (i, 0)),
                  pl.BlockSpec((1, d), lambda i, e: (0, 0))],
        out_specs=pl.BlockSpec((tb, d), lambda i, e: (i, 0)),
        out_shape=jax.ShapeDtypeStruct((t, d), F32),
        scratch_shapes=[pltpu.VMEM((tb, d), F32)],
        compiler_params=_cparams(("parallel", "arbitrary")),
        name="peer_dense",
    )(xn, ut, wmap, v, h2, norm_final.reshape(1, d))


def _alibi_slopes(n):
    return jnp.power(2.0, -8.0 * (jnp.arange(n, dtype=F32) + 1.0) / n)


def _prep_w_in(w_in):
    d = w_in.shape[0]
    wa = w_in[:, :COL_BQ]
    wq = (w_in[:, COL_BQ:COL_BKV] * (B_HD ** -0.5)).reshape(d, B_KV, B_REP, B_HD)
    wq = wq.transpose(0, 2, 1, 3).reshape(d, B_Q_COLS)
    wkv = w_in[:, COL_BKV:COL_BG]
    wg = w_in[:, COL_BG:COL_BG + B_GATE_COLS]
    pad = jnp.zeros((d, COL_M0 - COL_BG - B_GATE_COLS), w_in.dtype)
    wm = w_in[:, COL_BG + B_GATE_COLS:]
    return jnp.concatenate([wa, wq, wkv, wg, pad, wm], axis=1).astype(BF16)


def _block(x, mem, norm_mix, w_in, w_cmp_k1, w_cmp_k2, pe_cmp_k, w_cmp_v1, w_cmp_v2, pe_cmp_v,
           w_up_a, w_up_b, w_out, norm_x, norm_mem, w_xq, w_xkv, w_xo, norm_ffn, w_pq,
           sub_keys1, sub_keys2, expert_u, expert_v, norm_out, out_dtype):
    t, d = x.shape
    assert d == D_MODEL and w_in.shape[1] == COL_BG + B_GATE_COLS + 2 * D_MODEL
    tm = min(1024, t)

    proj = _norm_matmul(x, norm_mix, _prep_w_in(w_in), tm, 512, BF16)

    slopes_a = _alibi_slopes(A_GROUPS * A_HEADS)
    dil = [_dilated_group(proj, slopes_a, g) for g in range(A_GROUPS)]

    kc, vc = _nsa_compress(proj, pe_cmp_k, pe_cmp_v, w_cmp_k1, w_cmp_k2, w_cmp_v1, w_cmp_v2)
    yb = _nsa_attention(proj, kc, vc, _alibi_slopes(B_HEADS))

    wb = w_up_b.reshape(B_KV, B_REP, B_HD, d).transpose(1, 0, 2, 3).reshape(B_HEADS * B_HD, d)
    h1 = _merge(x, proj, [o for o, _ in dil], [l for _, l in dil], yb,
                w_up_a.astype(BF16), wb.astype(BF16), w_out.astype(BF16), min(512, t))

    kvm = _norm_matmul(mem, norm_mem, w_xkv.astype(BF16), mem.shape[0], 512, BF16)
    h2, xn = _cross(h1, norm_x, norm_ffn, w_xq.astype(BF16), kvm, w_xo.astype(BF16), min(512, t))

    wmap = _peer_route(xn, w_pq.astype(BF16), sub_keys1.astype(BF16), sub_keys2.astype(BF16))
    wmap = wmap.reshape(t, N_EXPERTS)
    return _peer_dense(xn, _cast_transpose(expert_u, 512, 1024), wmap, expert_v.astype(BF16), h2, norm_out,
                       min(512, t), 1024).astype(out_dtype)


@jax.jit
def kernel(x, mem, norm_mix, w_in, w_cmp_k1, w_cmp_k2, pe_cmp_k, w_cmp_v1, w_cmp_v2, pe_cmp_v, w_up_a, w_up_b,
           w_out, norm_x, norm_mem, w_xq, w_xkv, w_xo, norm_ffn, w_pq, sub_keys1, sub_keys2, expert_u,
           expert_v, norm_final):
    assert x.shape[0] == 1 and mem.shape[0] == 1 and norm_mix.shape[0] == 1
    out = _block(x[0], mem[0], norm_mix[0], w_in[0], w_cmp_k1[0], w_cmp_k2[0], pe_cmp_k[0], w_cmp_v1[0],
                 w_cmp_v2[0], pe_cmp_v[0], w_up_a[0], w_up_b[0], w_out[0], norm_x[0], norm_mem[0], w_xq[0],
                 w_xkv[0], w_xo[0], norm_ffn[0], w_pq[0], sub_keys1[0], sub_keys2[0], expert_u[0],
                 expert_v[0], norm_final, x.dtype)
    return out[None]
```

```python
import functools

import jax
import jax.numpy as jnp
from jax import lax
from jax.experimental import pallas as pl
from jax.experimental.pallas import tpu as pltpu

F32 = jnp.float32
BF16 = jnp.bfloat16

D_MODEL = 2048
BLK = 128
EPS = 1e-6
NEG = -1e30
TINY = 1e-30
FORCE = 1e4
REMOVED = -3e38

A_DILATIONS = (1, 4, 16)
A_WINDOWS = (128, 512, 2048)
A_GROUPS = 3
A_HEADS = 4
A_HD = 128
B_HEADS = 16
B_KV = 2
B_REP = B_HEADS // B_KV
B_HD = 64
CMP_LEN = 32
CMP_STRIDE = 16
CMP_HIDDEN = 128
SLC_LEN = 64
SLC_SHIFT = 6
N_SEL = 16
WIN = 512
X_HEADS = 4
X_HD = 128
PEER_HEADS = 8
PEER_DK = 256
N_KEYS = 128
N_EXPERTS = N_KEYS * N_KEYS
PEER_TOPK = 16

A_QKV_COLS = 3 * A_GROUPS * A_HEADS * A_HD
A_OUT = A_HEADS * A_HD
B_Q_COLS = B_HEADS * B_HD
B_KV_COLS = 3 * 2 * B_KV * B_HD
B_GATE_COLS = B_HEADS * 3

COL_A = 0
COL_BQ = COL_A + A_QKV_COLS
COL_BKV = COL_BQ + B_Q_COLS
COL_BG = COL_BKV + B_KV_COLS
COL_M0 = 6656
COL_M1 = COL_M0 + D_MODEL
PROJ_COLS = COL_M1 + D_MODEL

VMEM_LIMIT = 56 * 1024 * 1024

NT_DIMS = (((1,), (1,)), ((), ()))


def _cparams(sem):
    return pltpu.CompilerParams(dimension_semantics=sem, vmem_limit_bytes=VMEM_LIMIT)


def _rms(x, g):
    ms = jnp.mean(x * x, axis=-1, keepdims=True)
    return x * lax.rsqrt(ms + EPS) * g


_GELU_K1 = -2.0 * 0.7978845608028654 * 1.4426950408889634
_GELU_K2 = _GELU_K1 * 0.044715


def _gelu(x):
    return x / (1.0 + jnp.exp2(x * (_GELU_K1 + _GELU_K2 * (x * x))))


def _sigmoid(x):
    return 1.0 / (1.0 + jnp.exp(-x))


def _nmm_kernel(x_ref, g_ref, w_ref, o_ref, xn_ref):
    @pl.when(pl.program_id(1) == 0)
    def _():
        xn_ref[...] = _rms(x_ref[...].astype(F32), g_ref[...]).astype(BF16)

    o_ref[...] = jnp.dot(xn_ref[...], w_ref[...], preferred_element_type=F32).astype(o_ref.dtype)


def _norm_matmul(x, gain, w, tm, tn, out_dtype):
    t, d = x.shape
    n = w.shape[1]
    return pl.pallas_call(
        _nmm_kernel,
        grid=(t // tm, n // tn),
        in_specs=[pl.BlockSpec((tm, d), lambda i, j: (i, 0)),
                  pl.BlockSpec((1, d), lambda i, j: (0, 0)),
                  pl.BlockSpec((d, tn), lambda i, j: (0, j))],
        out_specs=pl.BlockSpec((tm, tn), lambda i, j: (i, j)),
        out_shape=jax.ShapeDtypeStruct((t, n), out_dtype),
        scratch_shapes=[pltpu.VMEM((tm, d), BF16)],
        compiler_params=_cparams(("parallel", "arbitrary")),
        name="norm_matmul",
    )(x, gain.reshape(1, d), w)


def _dil_kernel(slopes_ref, q_ref, kp_ref, ko_ref, vp_ref, vo_ref, o_ref, lse_ref, *, dil, qb, group):
    b = pl.program_id(1)
    nk = qb + BLK
    q = q_ref[...]
    k = jnp.concatenate([kp_ref[...], ko_ref[...]], axis=0)
    v = jnp.concatenate([vp_ref[...], vo_ref[...]], axis=0)
    i = lax.broadcasted_iota(jnp.int32, (qb, nk), 0)
    j = lax.broadcasted_iota(jnp.int32, (qb, nk), 1)
    rel = i + BLK - j
    jmin = jnp.where(b > 0, 0, BLK)
    ok = (rel >= 0) & (rel <= BLK) & (j >= jmin)
    dist = (rel * dil).astype(F32)
    scale = A_HD ** -0.5
    for h in range(A_HEADS):
        sl = slice(h * A_HD, (h + 1) * A_HD)
        s = lax.dot_general(q[:, sl], k[:, sl], NT_DIMS, preferred_element_type=F32) * scale
        s = s - slopes_ref[group * A_HEADS + h] * dist
        s = jnp.where(ok, s, NEG)
        m = jnp.max(s, axis=-1, keepdims=True)
        p = jnp.exp(s - m)
        den = jnp.sum(p, axis=-1, keepdims=True)
        o = jnp.dot(p.astype(BF16), v[:, sl], preferred_element_type=F32) / den
        o_ref[:, sl] = o
        lse_ref[:, sl] = jnp.broadcast_to(m + jnp.log(den), (qb, A_HD))


def _dilated_group(proj, slopes, group):
    t = proj.shape[0]
    dil = A_DILATIONS[group]
    assert A_WINDOWS[group] // dil == BLK
    l = t // dil
    qb = min(256, l)
    assert l % qb == 0 and qb % BLK == 0
    if dil == 1:
        src, stride, first, step = proj, PROJ_COLS // A_OUT, group, A_GROUPS
    else:
        mine = [proj[:, (s * A_GROUPS + group) * A_OUT:(s * A_GROUPS + group + 1) * A_OUT] for s in range(3)]
        src = jnp.concatenate(mine, axis=1).reshape(l, dil * 3 * A_OUT)
        stride, first, step = 3, 0, 1

    def col(s):
        return lambda r, b: (b, r * stride + first + s * step)

    def colp(s):
        return lambda r, b: (jnp.maximum(b * (qb // BLK) - 1, 0), r * stride + first + s * step)

    o, lse = pl.pallas_call(
        functools.partial(_dil_kernel, dil=dil, qb=qb, group=group),
        grid=(dil, l // qb),
        in_specs=[pl.BlockSpec(memory_space=pltpu.SMEM),
                  pl.BlockSpec((qb, A_OUT), col(0)),
                  pl.BlockSpec((BLK, A_OUT), colp(1)),
                  pl.BlockSpec((qb, A_OUT), col(1)),
                  pl.BlockSpec((BLK, A_OUT), colp(2)),
                  pl.BlockSpec((qb, A_OUT), col(2))],
        out_specs=[pl.BlockSpec((qb, A_OUT), lambda r, b: (b, r)),
                   pl.BlockSpec((qb, A_OUT), lambda r, b: (b, r))],
        out_shape=[jax.ShapeDtypeStruct((l, dil * A_OUT), F32),
                   jax.ShapeDtypeStruct((l, dil * A_OUT), F32)],
        compiler_params=_cparams(("parallel", "parallel")),
        name=f"dilated_attn_g{group}",
    )(slopes, src, src, src, src, src)
    return o.reshape(t, A_OUT), lse.reshape(t, A_OUT)


def _cmp_kernel(a_ref, pe_ref, w1_ref, w2_ref, o_ref):
    a = a_ref[0]
    n16 = a.shape[0]
    nxt = pltpu.roll(a, n16 - 1, axis=0)
    blocks = jnp.concatenate([a, nxt], axis=1) + pe_ref[0]
    hid = _gelu(jnp.dot(blocks.astype(BF16), w1_ref[0], preferred_element_type=F32))
    out = jnp.dot(hid.astype(BF16), w2_ref[0], preferred_element_type=F32)
    row = lax.broadcasted_iota(jnp.int32, out.shape, 0)
    o_ref[0] = jnp.where(row < n16 - 1, out, 0.0)


def _nsa_compress(proj, pe_k, pe_v, w_k1, w_k2, w_v1, w_v2):
    t = proj.shape[0]
    assert CMP_LEN == 2 * CMP_STRIDE and t % CMP_STRIDE == 0
    n16 = t // CMP_STRIDE
    kv = proj[:, COL_BKV:COL_BKV + 2 * B_KV * B_HD].astype(F32)
    a4 = kv.reshape(t, 2 * B_KV, B_HD).transpose(1, 0, 2).reshape(2 * B_KV, n16, CMP_STRIDE * B_HD)
    pe = jnp.stack([pe_k.reshape(1, -1), pe_v.reshape(1, -1)])
    w1 = jnp.stack([w_k1, w_v1]).astype(BF16)
    w2 = jnp.stack([w_k2, w_v2]).astype(BF16)
    kd = CMP_LEN * B_HD
    out = pl.pallas_call(
        _cmp_kernel,
        grid=(2 * B_KV,),
        in_specs=[pl.BlockSpec((1, n16, CMP_STRIDE * B_HD), lambda i: (i, 0, 0)),
                  pl.BlockSpec((1, 1, kd), lambda i: (i // B_KV, 0, 0)),
                  pl.BlockSpec((1, kd, CMP_HIDDEN), lambda i: (i // B_KV, 0, 0)),
                  pl.BlockSpec((1, CMP_HIDDEN, B_HD), lambda i: (i // B_KV, 0, 0))],
        out_specs=pl.BlockSpec((1, n16, B_HD), lambda i: (i, 0, 0)),
        out_shape=jax.ShapeDtypeStruct((2 * B_KV, n16, B_HD), F32),
        compiler_params=_cparams(("parallel",)),
        name="nsa_compress",
    )(a4, pe, w1, w2)
    kc = jnp.concatenate([out[0], out[1]], axis=-1).astype(BF16)
    vc = jnp.concatenate([out[2], out[3]], axis=-1).astype(BF16)
    return kc, vc


SEL_KT = 512
SEL_KT_SHIFT = 9
WIN_KEYS = WIN + BLK


def _lanes(col):
    return jnp.broadcast_to(col, (col.shape[0], 2 * B_HD))


def _tile_lanes(rep, n):
    return jnp.concatenate([rep] * (n // (2 * B_HD)), axis=1)


def _nsa_kernel(slopes_ref, q0_ref, q1_ref, gate_ref, kc_ref, vc_ref, ks_ref, vs_ref, kw_ref, vw_ref,
                ovt_ref, ex_ref, o_ref, qg_ref, oc_ref, ow_ref, acc_ref, m_ref, a_ref, z_ref, p_ref, ps_ref):
    b = pl.program_id(0)
    t0 = b * BLK
    rows = B_REP * BLK
    ncmp = kc_ref.shape[0]
    nslc = ovt_ref.shape[0]
    ntile_all = ex_ref.shape[0]
    lane = lax.broadcasted_iota(jnp.int32, (1, 2 * B_HD), 1)
    lo = lane < B_HD
    tq = t0 + lax.broadcasted_iota(jnp.int32, (BLK, 1), 0)
    tq_lane = t0 + lax.broadcasted_iota(jnp.int32, (1, BLK), 1)
    qscale = jnp.asarray(B_HD ** -0.5, BF16)
    qnat = [q0_ref[...] * qscale, q1_ref[...] * qscale]
    gates = _sigmoid(gate_ref[...].astype(F32))
    zero = jnp.zeros((), BF16)
    one = jnp.ones((), BF16)

    cend = lax.broadcasted_iota(jnp.int32, (1, ncmp), 1) * CMP_STRIDE + (CMP_LEN - 1)
    ok_c = cend <= tq
    dist_c = (t0 + BLK - 1 - cend).astype(F32)
    halves = [lo, jnp.logical_not(lo)]
    slopes = [[slopes_ref[g * B_REP + i] for i in range(B_REP)] for g in range(B_KV)]

    for g in range(B_KV):
        slope = slopes[g]
        for i in range(B_REP):
            x = qnat[g][:, (i // 2) * 2 * B_HD:(i // 2 + 1) * 2 * B_HD]
            if i % 2 != g:
                x = jnp.concatenate([x[:, B_HD:], x[:, :B_HD]], axis=1)
            qg_ref[g, i * BLK:(i + 1) * BLK, :] = jnp.where(halves[g], x, zero)
        s = lax.dot_general(qg_ref[g], kc_ref[...], NT_DIMS, preferred_element_type=F32)
        for i in range(B_REP):
            rs = slice(i * BLK, (i + 1) * BLK)
            z = jnp.where(ok_c, s[rs] - slope[i] * dist_c, NEG)
            z_ref[rs, :ncmp] = z
            m_ref[rs, :] = _lanes(jnp.max(z, axis=-1, keepdims=True))

        @pl.when(b >= 0)
        def _():
            for i in range(B_REP):
                rs = slice(i * BLK, (i + 1) * BLK)
                e = jnp.where(ok_c, jnp.exp(z_ref[rs, :ncmp] - _tile_lanes(m_ref[rs, :], ncmp)), 0.0)
                z_ref[rs, :ncmp] = e
                a_ref[rs, :] = _lanes(jnp.maximum(jnp.sum(e, axis=-1, keepdims=True), TINY))

        @pl.when(b >= 0)
        def _():
            psum = jnp.zeros((BLK, ncmp), F32)
            for i in range(B_REP):
                rs = slice(i * BLK, (i + 1) * BLK)
                p = z_ref[rs, :ncmp] / _tile_lanes(a_ref[rs, :], ncmp)
                psum = psum + p
                p_ref[rs, :ncmp] = p.astype(BF16)
            oc_ref[g] = jnp.dot(p_ref[:, :ncmp], vc_ref[...], preferred_element_type=F32)
            ps_ref[g] = psum

    imps = []
    for g in range(B_KV):
        psum = ps_ref[g]
        p_hi = psum.astype(BF16)
        p_lo = (psum - p_hi.astype(F32)).astype(BF16)
        imps.append(lax.dot_general(ovt_ref[...], p_hi, NT_DIMS, preferred_element_type=F32)
                    + lax.dot_general(ovt_ref[...], p_lo, NT_DIMS, preferred_element_type=F32))
    imp = jnp.concatenate(imps, axis=1)
    jj = lax.broadcasted_iota(jnp.int32, (nslc, B_KV * BLK), 0)
    jf = jj.astype(F32)
    cur = jnp.right_shift(jnp.concatenate([tq_lane] * B_KV, axis=1), SLC_SHIFT)
    forced = (jj == 0) | (jj == cur) | (jj == cur - 1)
    score = jnp.where(forced, FORCE, jnp.where(jj <= cur, imp, NEG))
    sel_t = jnp.zeros((nslc, B_KV * BLK), F32)
    for _ in range(min(N_SEL, nslc)):
        mx = jnp.max(score, axis=0, keepdims=True)
        idx = jnp.min(jnp.where(score == mx, jf, 1e9), axis=0, keepdims=True)
        pick = jf == idx
        sel_t = jnp.where(pick & (mx > NEG / 2), 1.0, sel_t)
        score = jnp.where(pick, REMOVED, score)
    ntp = max(8, ntile_all)
    tile_of = jnp.right_shift(lax.broadcasted_iota(jnp.int32, (ntp, nslc), 1), SEL_KT_SHIFT - SLC_SHIFT)
    krow = lax.broadcasted_iota(jnp.int32, (ntp, nslc), 0)
    gmat = jnp.where(tile_of == krow, 1.0, 0.0).astype(BF16)
    cnt = jnp.dot(gmat, sel_t.astype(BF16), preferred_element_type=F32)
    pow2 = jnp.left_shift(1, lax.broadcasted_iota(jnp.int32, (ntp, 1), 0)).astype(F32)
    sel_bs, bitss = [], []
    for g in range(B_KV):
        gl = slice(g * BLK, (g + 1) * BLK)
        sel_bs.append(sel_t[:, gl].T.astype(BF16))
        anyq = jnp.max(cnt[:, gl], axis=-1, keepdims=True)
        bitss.append(jnp.sum(jnp.where(anyq > 0.5, pow2, 0.0)).astype(jnp.int32))

    for g in range(B_KV):
        half, slope, sel_b, bits = halves[g], slopes[g], sel_bs[g], bitss[g]

        wstart = pl.multiple_of(jnp.maximum(t0 - WIN, 0), BLK)
        uw = wstart + lax.broadcasted_iota(jnp.int32, (1, WIN_KEYS), 1)
        dw = tq - uw
        ok_w = (dw >= 0) & (dw < WIN)
        dist_w = (t0 + BLK - 1 - uw).astype(F32)
        s = lax.dot_general(qg_ref[g], kw_ref[pl.ds(wstart, WIN_KEYS), :], NT_DIMS, preferred_element_type=F32)
        for i in range(B_REP):
            rs = slice(i * BLK, (i + 1) * BLK)
            z = jnp.where(ok_w, s[rs] - slope[i] * dist_w, NEG)
            z_ref[rs, :] = z
            m_ref[rs, :] = _lanes(jnp.max(z, axis=-1, keepdims=True))

        @pl.when(b >= 0)
        def _():
            for i in range(B_REP):
                rs = slice(i * BLK, (i + 1) * BLK)
                p_ref[rs, :] = jnp.exp(z_ref[rs, :] - _tile_lanes(m_ref[rs, :], WIN_KEYS)).astype(BF16)
            vw = jnp.where(half, vw_ref[pl.ds(wstart, WIN_KEYS), :], one)
            ow = jnp.dot(p_ref[...], vw, preferred_element_type=F32)
            ow_ref[...] = ow / pltpu.roll(ow, B_HD, axis=1)

        m_ref[...] = jnp.full((rows, 2 * B_HD), NEG, F32)
        acc_ref[...] = jnp.zeros((rows, 2 * B_HD), F32)
        ntiles = (t0 + BLK + SEL_KT - 1) // SEL_KT

        def tile_body(it, carry):
            kt = ntiles - 1 - it

            @pl.when(jnp.bitwise_and(jnp.right_shift(bits, kt), 1) == 1)
            def _():
                k0 = pl.multiple_of(kt * SEL_KT, SEL_KT)
                us = k0 + lax.broadcasted_iota(jnp.int32, (1, SEL_KT), 1)
                selx = jnp.dot(sel_b, ex_ref[kt], preferred_element_type=F32)
                ok_s = (selx > 0.5) & (us <= tq)
                dist_s = (t0 + BLK - 1 - us).astype(F32)
                s = lax.dot_general(qg_ref[g], ks_ref[pl.ds(k0, SEL_KT), :], NT_DIMS, preferred_element_type=F32)
                for i in range(B_REP):
                    rs = slice(i * BLK, (i + 1) * BLK)
                    z = jnp.where(ok_s, s[rs] - slope[i] * dist_s, NEG)
                    z_ref[rs, :SEL_KT] = z
                    m_old = m_ref[rs, :]
                    m_new = jnp.maximum(m_old, _lanes(jnp.max(z, axis=-1, keepdims=True)))
                    a_ref[rs, :] = jnp.exp(m_old - m_new)
                    m_ref[rs, :] = m_new

                @pl.when(kt >= 0)
                def _():
                    for i in range(B_REP):
                        rs = slice(i * BLK, (i + 1) * BLK)
                        p_ref[rs, :SEL_KT] = jnp.exp(z_ref[rs, :SEL_KT]
                                                     - _tile_lanes(m_ref[rs, :], SEL_KT)).astype(BF16)
                    vs = jnp.where(half, vs_ref[pl.ds(k0, SEL_KT), :], one)
                    acc_ref[...] = a_ref[...] * acc_ref[...] + jnp.dot(p_ref[:, :SEL_KT], vs,
                                                                       preferred_element_type=F32)

            return carry

        lax.fori_loop(0, ntiles, tile_body, 0)

        for i in range(B_REP):
            rs = slice(i * BLK, (i + 1) * BLK)
            h = g * B_REP + i
            acc = acc_ref[rs, :]
            o_s = acc / pltpu.roll(acc, B_HD, axis=1)
            out = (gates[:, 3 * h:3 * h + 1] * oc_ref[g, rs, :] + gates[:, 3 * h + 1:3 * h + 2] * o_s
                   + gates[:, 3 * h + 2:3 * h + 3] * ow_ref[rs, :])
            cs = slice(i * 2 * B_HD, (i + 1) * 2 * B_HD)
            if g == 0:
                o_ref[:, cs] = out.astype(o_ref.dtype)
            else:
                o_ref[:, cs] = jnp.where(lo, o_ref[:, cs], out.astype(o_ref.dtype))


def _nsa_attention(proj, kc, vc, slopes):
    t = proj.shape[0]
    assert t % SEL_KT == 0 and SLC_LEN == 1 << SLC_SHIFT and SEL_KT == 1 << SEL_KT_SHIFT
    ncmp = kc.shape[0]
    nslc = t // SLC_LEN
    ntile = t // SEL_KT
    assert ntile <= 24
    c_start = jnp.arange(ncmp)[None, :] * CMP_STRIDE
    s_start = jnp.arange(nslc)[:, None] * SLC_LEN
    overlap = jnp.clip(jnp.minimum(c_start + CMP_LEN, s_start + SLC_LEN) - jnp.maximum(c_start, s_start),
                       0, None).astype(F32) / CMP_LEN
    overlap = jnp.where(jnp.arange(ncmp)[None, :] < ncmp - 1, overlap, 0.0).astype(BF16)
    key_blk = (jnp.arange(ntile)[:, None, None] * SEL_KT + jnp.arange(SEL_KT)[None, None, :]) // SLC_LEN
    expand = (key_blk == jnp.arange(nslc)[None, :, None]).astype(BF16)
    w2 = 2 * B_HD
    qcol = COL_BQ // (4 * w2)
    kvcol = COL_BKV // w2
    full = lambda c: pl.BlockSpec((t, w2), lambda b: (0, c))
    rows = B_REP * BLK
    return pl.pallas_call(
        _nsa_kernel,
        grid=(t // BLK,),
        in_specs=[pl.BlockSpec(memory_space=pltpu.SMEM),
                  pl.BlockSpec((BLK, 4 * w2), lambda b: (b, qcol)),
                  pl.BlockSpec((BLK, 4 * w2), lambda b: (b, qcol + 1)),
                  pl.BlockSpec((BLK, w2), lambda b: (b, COL_BG // w2)),
                  pl.BlockSpec((ncmp, w2), lambda b: (0, 0)),
                  pl.BlockSpec((ncmp, w2), lambda b: (0, 0)),
                  full(kvcol + 2), full(kvcol + 3), full(kvcol + 4), full(kvcol + 5),
                  pl.BlockSpec((nslc, ncmp), lambda b: (0, 0)),
                  pl.BlockSpec((ntile, nslc, SEL_KT), lambda b: (0, 0, 0))],
        out_specs=pl.BlockSpec((BLK, B_HEADS * B_HD), lambda b: (b, 0)),
        out_shape=jax.ShapeDtypeStruct((t, B_HEADS * B_HD), BF16),
        scratch_shapes=[pltpu.VMEM((B_KV, rows, w2), BF16),
                        pltpu.VMEM((B_KV, rows, w2), F32),
                        pltpu.VMEM((rows, w2), F32),
                        pltpu.VMEM((rows, w2), F32),
                        pltpu.VMEM((rows, w2), F32),
                        pltpu.VMEM((rows, w2), F32),
                        pltpu.VMEM((rows, WIN_KEYS), F32),
                        pltpu.VMEM((rows, WIN_KEYS), BF16),
                        pltpu.VMEM((B_KV, BLK, ncmp), F32)],
        compiler_params=_cparams(("parallel",)),
        name="nsa_attention",
    )(slopes, proj, proj, proj, kc, vc, proj, proj, proj, proj, overlap, expand)


MERGE_KC = 512


def _merge_kernel(o0_ref, o1_ref, o2_ref, l0_ref, l1_ref, l2_ref, yb_ref, m0_ref, m1_ref, x_ref,
                  wa_ref, wb_ref, wo_ref, h_ref, ya_ref):
    kc = pl.program_id(1)

    @pl.when(kc == 0)
    def _():
        l0, l1, l2 = l0_ref[...], l1_ref[...], l2_ref[...]
        m = jnp.maximum(jnp.maximum(l0, l1), l2)
        e0, e1, e2 = jnp.exp(l0 - m), jnp.exp(l1 - m), jnp.exp(l2 - m)
        den = e0 + e1 + e2
        ya = (e0 / den) * o0_ref[...] + (e1 / den) * o1_ref[...] + (e2 / den) * o2_ref[...]
        ya_ref[...] = ya.astype(BF16)
        h_ref[...] = x_ref[...]

    ua = jnp.dot(ya_ref[...], wa_ref[...], preferred_element_type=F32)
    ub = jnp.dot(yb_ref[...], wb_ref[...], preferred_element_type=F32)
    mix = _sigmoid(m0_ref[...].astype(F32)) * ua + _sigmoid(m1_ref[...].astype(F32)) * ub
    h_ref[...] += jnp.dot(mix.astype(BF16), wo_ref[...], preferred_element_type=F32)


def _merge(x, proj, dil_o, dil_lse, yb, wa, wb, wo, tm):
    t, d = x.shape
    nk = d // MERGE_KC
    row = lambda w: pl.BlockSpec((tm, w), lambda i, k: (i, 0))
    return pl.pallas_call(
        _merge_kernel,
        grid=(t // tm, nk),
        in_specs=[row(A_OUT)] * 6 + [row(B_HEADS * B_HD),
                  pl.BlockSpec((tm, MERGE_KC), lambda i, k: (i, COL_M0 // MERGE_KC + k)),
                  pl.BlockSpec((tm, MERGE_KC), lambda i, k: (i, COL_M1 // MERGE_KC + k)),
                  row(d),
                  pl.BlockSpec((A_OUT, MERGE_KC), lambda i, k: (0, k)),
                  pl.BlockSpec((B_HEADS * B_HD, MERGE_KC), lambda i, k: (0, k)),
                  pl.BlockSpec((MERGE_KC, d), lambda i, k: (k, 0))],
        out_specs=pl.BlockSpec((tm, d), lambda i, k: (i, 0)),
        out_shape=jax.ShapeDtypeStruct((t, d), F32),
        scratch_shapes=[pltpu.VMEM((tm, A_OUT), BF16)],
        compiler_params=_cparams(("parallel", "arbitrary")),
        name="merge",
    )(*dil_o, *dil_lse, yb, proj, proj, x, wa, wb, wo)


def _cross_kernel(h_ref, gx_ref, gf_ref, wq_ref, kv_ref, wo_ref, h2_ref, xn_ref):
    h = h_ref[...]
    hn = _rms(h, gx_ref[...]).astype(BF16)
    q = jnp.dot(hn, wq_ref[...], preferred_element_type=F32).astype(BF16)
    kv = kv_ref[...]
    nh = X_HEADS * X_HD
    outs = []
    for hd in range(X_HEADS):
        sl = slice(hd * X_HD, (hd + 1) * X_HD)
        s = lax.dot_general(q[:, sl], kv[:, sl], NT_DIMS, preferred_element_type=F32) * (X_HD ** -0.5)
        m = jnp.max(s, axis=-1, keepdims=True)
        p = jnp.exp(s - m)
        p = p / jnp.sum(p, axis=-1, keepdims=True)
        outs.append(jnp.dot(p.astype(BF16), kv[:, nh + hd * X_HD:nh + (hd + 1) * X_HD],
                            preferred_element_type=F32))
    o = jnp.concatenate(outs, axis=1).astype(BF16)
    h2 = h + jnp.dot(o, wo_ref[...], preferred_element_type=F32)
    h2_ref[...] = h2
    xn_ref[...] = _rms(h2, gf_ref[...]).astype(BF16)


def _cross(h1, norm_x, norm_ffn, wq, kv, wo, tm):
    t, d = h1.shape
    nm = kv.shape[0]
    nh = X_HEADS * X_HD
    const = lambda shape: pl.BlockSpec(shape, lambda i: (0, 0))
    return pl.pallas_call(
        _cross_kernel,
        grid=(t // tm,),
        in_specs=[pl.BlockSpec((tm, d), lambda i: (i, 0)), const((1, d)), const((1, d)),
                  const((d, nh)), const((nm, 2 * nh)), const((nh, d))],
        out_specs=[pl.BlockSpec((tm, d), lambda i: (i, 0)), pl.BlockSpec((tm, d), lambda i: (i, 0))],
        out_shape=[jax.ShapeDtypeStruct((t, d), F32), jax.ShapeDtypeStruct((t, d), BF16)],
        compiler_params=_cparams(("parallel",)),
        name="cross_attn",
    )(h1, norm_x.reshape(1, d), norm_ffn.reshape(1, d), wq, kv, wo)


ROUTE_TB = 128
ROUTE_HPI = 4
MAP_TOK = 16


def _top16_rows(s, rowf, nrows):
    vals, idxs = [], []
    for _ in range(PEER_TOPK):
        mx = jnp.max(s, axis=0, keepdims=True)
        idx = jnp.min(jnp.where(s == mx, rowf, float(nrows)), axis=0, keepdims=True)
        vals.append(mx)
        idxs.append(idx)
        s = jnp.where(rowf == idx, REMOVED, s)
    return vals, idxs


def _route_kernel(xn_ref, wq_ref, k1_ref, k2_ref, w_ref, q_scr, g_scr, e1_scr, e2_scr, gt_scr, e1t_scr, e2t_scr,
                  wt_scr):
    tb = xn_ref.shape[0]
    half = PEER_DK // 2
    q = jnp.dot(xn_ref[...], wq_ref[...], preferred_element_type=F32).astype(BF16)
    for c in range(2 * PEER_HEADS):
        q_scr[c] = q[:, c * half:(c + 1) * half]
    nl = ROUTE_HPI * tb
    rowf = lax.broadcasted_iota(jnp.int32, (N_KEYS, nl), 0).astype(F32)
    nc = 8 * 10
    rowc = lax.broadcasted_iota(jnp.int32, (nc, nl), 0).astype(F32)

    def head_body(hp, carry):
        h0 = hp * ROUTE_HPI
        s1 = jnp.concatenate([lax.dot_general(k1_ref[...], q_scr[2 * (h0 + u)], NT_DIMS, preferred_element_type=F32)
                              for u in range(ROUTE_HPI)], axis=1)
        s2 = jnp.concatenate([lax.dot_general(k2_ref[...], q_scr[2 * (h0 + u) + 1], NT_DIMS,
                                              preferred_element_type=F32) for u in range(ROUTE_HPI)], axis=1)
        v1, i1 = _top16_rows(s1, rowf, N_KEYS)
        v2, i2 = _top16_rows(s2, rowf, N_KEYS)
        sub8 = lax.broadcasted_iota(jnp.int32, (8, nl), 0)
        v2lo, v2hi = jnp.concatenate(v2[:8], axis=0), jnp.concatenate(v2[8:], axis=0)
        i2lo, i2hi = jnp.concatenate(i2[:8], axis=0), jnp.concatenate(i2[8:], axis=0)
        cand_p = [v1[0] + v2lo, v1[0] + v2hi]
        e1_p = [jnp.broadcast_to(i1[0], (8, nl))] * 2
        e2_p = [i2lo, i2hi]
        for a in range(1, 8):
            nb = PEER_TOPK // (a + 1)
            piece = v1[a] + v2lo
            cand_p.append(piece if nb >= 8 else jnp.where(sub8 < nb, piece, REMOVED))
            e1_p.append(jnp.broadcast_to(i1[a], (8, nl)))
            e2_p.append(i2lo)
        cand_p.append(jnp.concatenate(v1[8:], axis=0) + v2[0])
        e1_p.append(jnp.concatenate(i1[8:], axis=0))
        e2_p.append(jnp.broadcast_to(i2[0], (8, nl)))
        cand = jnp.concatenate(cand_p, axis=0)
        e1c = jnp.concatenate(e1_p, axis=0)
        e2c = jnp.concatenate(e2_p, axis=0)
        vs, e1, e2 = [], [], []
        for _ in range(PEER_TOPK):
            mx = jnp.max(cand, axis=0, keepdims=True)
            idx = jnp.min(jnp.where(cand == mx, rowc, float(nc)), axis=0, keepdims=True)
            pick = rowc == idx
            vs.append(mx)
            e1.append(jnp.sum(jnp.where(pick, e1c, 0.0), axis=0, keepdims=True))
            e2.append(jnp.sum(jnp.where(pick, e2c, 0.0), axis=0, keepdims=True))
            cand = jnp.where(pick, REMOVED, cand)
        vsa = jnp.concatenate(vs, axis=0)
        ex = jnp.exp(vsa - vs[0])
        gate = ex / jnp.sum(ex, axis=0, keepdims=True)
        e1a, e2a = jnp.concatenate(e1, axis=0), jnp.concatenate(e2, axis=0)
        for u in range(ROUTE_HPI):
            r0 = pl.multiple_of((h0 + u) * PEER_TOPK, PEER_TOPK)
            ls = slice(u * tb, (u + 1) * tb)
            g_scr[pl.ds(r0, PEER_TOPK), :] = gate[:, ls]
            e1_scr[pl.ds(r0, PEER_TOPK), :] = e1a[:, ls]
            e2_scr[pl.ds(r0, PEER_TOPK), :] = e2a[:, ls]
        return carry

    lax.fori_loop(0, PEER_HEADS // ROUTE_HPI, head_body, 0)

    gt_scr[...] = g_scr[...].T
    e1t_scr[...] = e1_scr[...].T
    e2t_scr[...] = e2_scr[...].T
    sub = lax.broadcasted_iota(jnp.int32, (N_KEYS, PEER_HEADS * PEER_TOPK), 0).astype(F32)

    def group_body(grp, carry):
        shape = (N_KEYS, PEER_HEADS * PEER_TOPK)
        for tt in range(MAP_TOK):
            t = grp * MAP_TOK + tt
            g_row = jnp.broadcast_to(gt_scr[pl.ds(t, 1), :], shape)
            e1_row = jnp.broadcast_to(e1t_scr[pl.ds(t, 1), :], shape)
            e2_row = jnp.broadcast_to(e2t_scr[pl.ds(t, 1), :], shape)
            a = jnp.where(e1_row == sub, g_row, 0.0).astype(BF16)
            bm = jnp.where(e2_row == sub, 1.0, 0.0).astype(BF16)
            wt_scr[pl.ds(tt, N_KEYS, stride=MAP_TOK), :] = lax.dot_general(a, bm, NT_DIMS,
                                                                          preferred_element_type=F32)
        w_ref[grp] = wt_scr[...].astype(w_ref.dtype)
        return carry

    lax.fori_loop(0, tb // MAP_TOK, group_body, 0)


def _peer_route(xn, wq, k1, k2):
    t, d = xn.shape
    tb = ROUTE_TB
    half = PEER_DK // 2
    slots = PEER_HEADS * PEER_TOPK
    const = lambda shape: pl.BlockSpec(shape, lambda i: (0,) * len(shape))
    return pl.pallas_call(
        _route_kernel,
        grid=(t // tb,),
        in_specs=[pl.BlockSpec((tb, d), lambda i: (i, 0)), const((d, PEER_HEADS * PEER_DK)),
                  const((N_KEYS, half)), const((N_KEYS, half))],
        out_specs=pl.BlockSpec((tb // MAP_TOK, N_KEYS * MAP_TOK, N_KEYS), lambda i: (i, 0, 0)),
        out_shape=jax.ShapeDtypeStruct((t // MAP_TOK, N_KEYS * MAP_TOK, N_KEYS), BF16),
        scratch_shapes=[pltpu.VMEM((2 * PEER_HEADS, tb, half), BF16),
                        pltpu.VMEM((slots, tb), F32), pltpu.VMEM((slots, tb), F32), pltpu.VMEM((slots, tb), F32),
                        pltpu.VMEM((tb, slots), F32), pltpu.VMEM((tb, slots), F32), pltpu.VMEM((tb, slots), F32),
                        pltpu.VMEM((N_KEYS * MAP_TOK, N_KEYS), F32)],
        compiler_params=_cparams(("parallel",)),
        name="peer_route",
    )(xn, wq, k1, k2)


def _castT_kernel(u_ref, o_ref):
    o_ref[...] = u_ref[...].T.astype(o_ref.dtype)


def _cast_transpose(u, te, td):
    ne, d = u.shape
    return pl.pallas_call(
        _castT_kernel,
        grid=(ne // te, d // td),
        in_specs=[pl.BlockSpec((te, td), lambda i, j: (i, j))],
        out_specs=pl.BlockSpec((td, te), lambda i, j: (j, i)),
        out_shape=jax.ShapeDtypeStruct((d, ne), BF16),
        compiler_params=_cparams(("parallel", "parallel")),
        name="expert_u_transpose",
    )(u)


def _dense_kernel(xn_ref, ut_ref, w_ref, v_ref, h_ref, gf_ref, o_ref, acc_ref, aw_ref):
    e = pl.program_id(1)

    @pl.when(e == 0)
    def _():
        acc_ref[...] = jnp.zeros_like(acc_ref)

    tb, eb = aw_ref.shape
    a = _gelu(jnp.dot(xn_ref[...], ut_ref[...], preferred_element_type=F32))
    for bb in range(tb // MAP_TOK):
        rs = slice(bb * MAP_TOK, (bb + 1) * MAP_TOK)
        for j in range(eb // N_KEYS):
            cs = slice(j * N_KEYS, (j + 1) * N_KEYS)
            aw_ref[rs, cs] = (a[rs, cs] * w_ref[bb, j * MAP_TOK:(j + 1) * MAP_TOK, :].astype(F32)).astype(BF16)
    acc_ref[...] += jnp.dot(aw_ref[...], v_ref[...], preferred_element_type=F32)

    @pl.when(e == pl.num_programs(1) - 1)
    def _():
        o_ref[...] = _rms(h_ref[...] + acc_ref[...], gf_ref[...]).astype(o_ref.dtype)


def _peer_dense(xn, ut, wmap, v, h2, norm_final, tb, eb):
    t, d = xn.shape
    ne = ut.shape[1]
    return pl.pallas_call(
        _dense_kernel,
        grid=(t // tb, ne // eb),
        in_specs=[pl.BlockSpec((tb, d), lambda i, e: (i, 0)),
                  pl.BlockSpec((d, eb), lambda i, e: (0, e)),
                  pl.BlockSpec((tb // MAP_TOK, eb // N_KEYS * MAP_TOK, N_KEYS), lambda i, e: (i, e, 0)),
                  pl.BlockSpec((eb, d), lambda i, e: (e, 0)),
                  pl.BlockSpec((tb, d), lambda i, e: (i, 0)),
                  pl.BlockSpec((1, d), lambda i, e: (0, 0))],
        out_specs=pl.BlockSpec((tb, d), lambda i, e: (i, 0)),
        out_shape=jax.ShapeDtypeStruct((t, d), F32),
        scratch_shapes=[pltpu.VMEM((tb, d), F32), pltpu.VMEM((tb, eb), BF16)],
        compiler_params=_cparams(("parallel", "arbitrary")),
        name="peer_dense",
    )(xn, ut, wmap, v, h2, norm_final.reshape(1, d))


def _alibi_slopes(n):
    return jnp.power(2.0, -8.0 * (jnp.arange(n, dtype=F32) + 1.0) / n)


def _prep_w_in(w_in):
    d = w_in.shape[0]
    split = COL_BG + B_GATE_COLS
    pad = jnp.zeros((d, COL_M0 - split), BF16)
    return jnp.concatenate([w_in[:, :split].astype(BF16), pad, w_in[:, split:].astype(BF16)], axis=1)


def _block(x, mem, norm_mix, w_in, w_cmp_k1, w_cmp_k2, pe_cmp_k, w_cmp_v1, w_cmp_v2, pe_cmp_v,
           w_up_a, w_up_b, w_out, norm_x, norm_mem, w_xq, w_xkv, w_xo, norm_ffn, w_pq,
           sub_keys1, sub_keys2, expert_u, expert_v, norm_out, out_dtype):
    t, d = x.shape
    assert d == D_MODEL and w_in.shape[1] == COL_BG + B_GATE_COLS + 2 * D_MODEL
    tm = min(1024, t)

    proj = _norm_matmul(x, norm_mix, _prep_w_in(w_in), tm, 512, BF16)

    slopes_a = _alibi_slopes(A_GROUPS * A_HEADS)
    dil = [_dilated_group(proj, slopes_a, g) for g in range(A_GROUPS)]

    kc, vc = _nsa_compress(proj, pe_cmp_k, pe_cmp_v, w_cmp_k1, w_cmp_k2, w_cmp_v1, w_cmp_v2)
    yb = _nsa_attention(proj, kc, vc, _alibi_slopes(B_HEADS))

    wb = w_up_b.reshape(B_KV, B_REP, B_HD, d).transpose(1, 0, 2, 3).reshape(B_HEADS * B_HD, d)
    h1 = _merge(x, proj, [o for o, _ in dil], [l for _, l in dil], yb,
                w_up_a.astype(BF16), wb.astype(BF16), w_out.astype(BF16), min(512, t))

    kvm = _norm_matmul(mem, norm_mem, w_xkv.astype(BF16), mem.shape[0], 512, BF16)
    h2, xn = _cross(h1, norm_x, norm_ffn, w_xq.astype(BF16), kvm, w_xo.astype(BF16), min(512, t))

    wmap = _peer_route(xn, w_pq.astype(BF16), sub_keys1.astype(BF16), sub_keys2.astype(BF16))
    return _peer_dense(xn, _cast_transpose(expert_u, 512, 1024), wmap, expert_v.astype(BF16), h2, norm_out,
                       min(512, t), 1024).astype(out_dtype)


@jax.jit
def kernel(x, mem, norm_mix, w_in, w_cmp_k1, w_cmp_k2, pe_cmp_k, w_cmp_v1, w_cmp_v2, pe_cmp_v, w_up_a, w_up_b,
           w_out, norm_x, norm_mem, w_xq, w_xkv, w_xo, norm_ffn, w_pq, sub_keys1, sub_keys2, expert_u,
           expert_v, norm_final):
    assert x.shape[0] == 1 and mem.shape[0] == 1 and norm_mix.shape[0] == 1
    out = _block(x[0], mem[0], norm_mix[0], w_in[0], w_cmp_k1[0], w_cmp_k2[0], pe_cmp_k[0], w_cmp_v1[0],
                 w_cmp_v2[0], pe_cmp_v[0], w_up_a[0], w_up_b[0], w_out[0], norm_x[0], norm_mem[0], w_xq[0],
                 w_xkv[0], w_xo[0], norm_ffn[0], w_pq[0], sub_keys1[0], sub_keys2[0], expert_u[0],
                 expert_v[0], norm_final, x.dtype)
    return out[None]
```

```python
import functools

import jax
import jax.numpy as jnp
from jax import lax
from jax.experimental import pallas as pl
from jax.experimental.pallas import tpu as pltpu

F32 = jnp.float32
BF16 = jnp.bfloat16

D_MODEL = 2048
BLK = 128
EPS = 1e-6
NEG = -1e30
TINY = 1e-30
FORCE = 1e4
REMOVED = -3e38

A_DILATIONS = (1, 4, 16)
A_WINDOWS = (128, 512, 2048)
A_GROUPS = 3
A_HEADS = 4
A_HD = 128
B_HEADS = 16
B_KV = 2
B_REP = B_HEADS // B_KV
B_HD = 64
CMP_LEN = 32
CMP_STRIDE = 16
CMP_HIDDEN = 128
SLC_LEN = 64
SLC_SHIFT = 6
N_SEL = 16
WIN = 512
X_HEADS = 4
X_HD = 128
PEER_HEADS = 8
PEER_DK = 256
N_KEYS = 128
N_EXPERTS = N_KEYS * N_KEYS
PEER_TOPK = 16

A_QKV_COLS = 3 * A_GROUPS * A_HEADS * A_HD
A_OUT = A_HEADS * A_HD
B_Q_COLS = B_HEADS * B_HD
B_KV_COLS = 3 * 2 * B_KV * B_HD
B_GATE_COLS = B_HEADS * 3

COL_A = 0
COL_BQ = COL_A + A_QKV_COLS
COL_BKV = COL_BQ + B_Q_COLS
COL_BG = COL_BKV + B_KV_COLS
PROJ_COLS = 6656

VMEM_LIMIT = 56 * 1024 * 1024

NT_DIMS = (((1,), (1,)), ((), ()))


def _cparams(sem):
    return pltpu.CompilerParams(dimension_semantics=sem, vmem_limit_bytes=VMEM_LIMIT)


def _rms(x, g):
    ms = jnp.mean(x * x, axis=-1, keepdims=True)
    return x * lax.rsqrt(ms + EPS) * g


_GELU_K1 = -2.0 * 0.7978845608028654 * 1.4426950408889634
_GELU_K2 = _GELU_K1 * 0.044715


def _gelu(x):
    return x / (1.0 + jnp.exp2(x * (_GELU_K1 + _GELU_K2 * (x * x))))


def _sigmoid(x):
    return 1.0 / (1.0 + jnp.exp(-x))


def _nmm_kernel(x_ref, g_ref, w_ref, o_ref, xn_ref):
    @pl.when(pl.program_id(1) == 0)
    def _():
        xn_ref[...] = _rms(x_ref[...].astype(F32), g_ref[...]).astype(BF16)

    o_ref[...] = jnp.dot(xn_ref[...], w_ref[...], preferred_element_type=F32).astype(o_ref.dtype)


def _norm_matmul(x, gain, w, tm, tn, out_dtype):
    t, d = x.shape
    n = w.shape[1]
    return pl.pallas_call(
        _nmm_kernel,
        grid=(t // tm, n // tn),
        in_specs=[pl.BlockSpec((tm, d), lambda i, j: (i, 0)),
                  pl.BlockSpec((1, d), lambda i, j: (0, 0)),
                  pl.BlockSpec((d, tn), lambda i, j: (0, j))],
        out_specs=pl.BlockSpec((tm, tn), lambda i, j: (i, j)),
        out_shape=jax.ShapeDtypeStruct((t, n), out_dtype),
        scratch_shapes=[pltpu.VMEM((tm, d), BF16)],
        compiler_params=_cparams(("parallel", "arbitrary")),
        name="norm_matmul",
    )(x, gain.reshape(1, d), w)


def _dil_kernel(slopes_ref, q_ref, kp_ref, ko_ref, vp_ref, vo_ref, o_ref, lse_ref, z_ref, m_ref, *, dil, qb, group):
    b = pl.program_id(1)
    nk = qb + BLK
    q = q_ref[...]
    k = jnp.concatenate([kp_ref[...], ko_ref[...]], axis=0)
    i = lax.broadcasted_iota(jnp.int32, (qb, nk), 0)
    j = lax.broadcasted_iota(jnp.int32, (qb, nk), 1)
    rel = i + BLK - j
    jmin = jnp.where(b > 0, 0, BLK)
    ok = (rel >= 0) & (rel <= BLK) & (j >= jmin)
    dist = (rel * dil).astype(F32)
    scale = A_HD ** -0.5
    for h in range(A_HEADS):
        sl = slice(h * A_HD, (h + 1) * A_HD)
        s = lax.dot_general(q[:, sl], k[:, sl], NT_DIMS, preferred_element_type=F32) * scale
        s = jnp.where(ok, s - slopes_ref[group * A_HEADS + h] * dist, NEG)
        z_ref[h] = s
        m_ref[h] = jnp.broadcast_to(jnp.max(s, axis=-1, keepdims=True), (qb, A_HD))

    @pl.when(b >= 0)
    def _():
        v = jnp.concatenate([vp_ref[...], vo_ref[...]], axis=0)
        for h in range(A_HEADS):
            sl = slice(h * A_HD, (h + 1) * A_HD)
            m = m_ref[h]
            p = jnp.exp(z_ref[h] - jnp.concatenate([m] * (nk // A_HD), axis=1))
            den = jnp.sum(p, axis=-1, keepdims=True)
            o_ref[:, sl] = jnp.dot(p.astype(BF16), v[:, sl], preferred_element_type=F32) / den
            lse_ref[:, sl] = m + jnp.log(den)


def _dilated_group(proj, slopes, group):
    t = proj.shape[0]
    dil = A_DILATIONS[group]
    assert A_WINDOWS[group] // dil == BLK
    l = t // dil
    qb = min(256, l)
    assert l % qb == 0 and qb % BLK == 0
    if dil == 1:
        src, stride, first, step = proj, PROJ_COLS // A_OUT, group, A_GROUPS
    else:
        mine = [proj[:, (s * A_GROUPS + group) * A_OUT:(s * A_GROUPS + group + 1) * A_OUT] for s in range(3)]
        src = jnp.concatenate(mine, axis=1).reshape(l, dil * 3 * A_OUT)
        stride, first, step = 3, 0, 1

    def col(s):
        return lambda r, b: (b, r * stride + first + s * step)

    def colp(s):
        return lambda r, b: (jnp.maximum(b * (qb // BLK) - 1, 0), r * stride + first + s * step)

    o, lse = pl.pallas_call(
        functools.partial(_dil_kernel, dil=dil, qb=qb, group=group),
        grid=(dil, l // qb),
        in_specs=[pl.BlockSpec(memory_space=pltpu.SMEM),
                  pl.BlockSpec((qb, A_OUT), col(0)),
                  pl.BlockSpec((BLK, A_OUT), colp(1)),
                  pl.BlockSpec((qb, A_OUT), col(1)),
                  pl.BlockSpec((BLK, A_OUT), colp(2)),
                  pl.BlockSpec((qb, A_OUT), col(2))],
        out_specs=[pl.BlockSpec((qb, A_OUT), lambda r, b: (b, r)),
                   pl.BlockSpec((qb, A_OUT), lambda r, b: (b, r))],
        out_shape=[jax.ShapeDtypeStruct((l, dil * A_OUT), F32),
                   jax.ShapeDtypeStruct((l, dil * A_OUT), F32)],
        scratch_shapes=[pltpu.VMEM((A_HEADS, qb, qb + BLK), F32), pltpu.VMEM((A_HEADS, qb, A_HD), F32)],
        compiler_params=_cparams(("parallel", "parallel")),
        name=f"dilated_attn_g{group}",
    )(slopes, src, src, src, src, src)
    return o.reshape(t, A_OUT), lse.reshape(t, A_OUT)


def _cmp_kernel(a_ref, pe_ref, w1_ref, w2_ref, o_ref):
    a = a_ref[0]
    n16 = a.shape[0]
    nxt = pltpu.roll(a, n16 - 1, axis=0)
    blocks = jnp.concatenate([a, nxt], axis=1) + pe_ref[0]
    hid = _gelu(jnp.dot(blocks.astype(BF16), w1_ref[0], preferred_element_type=F32))
    out = jnp.dot(hid.astype(BF16), w2_ref[0], preferred_element_type=F32)
    row = lax.broadcasted_iota(jnp.int32, out.shape, 0)
    o_ref[0] = jnp.where(row < n16 - 1, out, 0.0)


def _nsa_compress(proj, pe_k, pe_v, w_k1, w_k2, w_v1, w_v2):
    t = proj.shape[0]
    assert CMP_LEN == 2 * CMP_STRIDE and t % CMP_STRIDE == 0
    n16 = t // CMP_STRIDE
    kv = proj[:, COL_BKV:COL_BKV + 2 * B_KV * B_HD].astype(F32)
    a4 = kv.reshape(t, 2 * B_KV, B_HD).transpose(1, 0, 2).reshape(2 * B_KV, n16, CMP_STRIDE * B_HD)
    pe = jnp.stack([pe_k.reshape(1, -1), pe_v.reshape(1, -1)])
    w1 = jnp.stack([w_k1, w_v1]).astype(BF16)
    w2 = jnp.stack([w_k2, w_v2]).astype(BF16)
    kd = CMP_LEN * B_HD
    out = pl.pallas_call(
        _cmp_kernel,
        grid=(2 * B_KV,),
        in_specs=[pl.BlockSpec((1, n16, CMP_STRIDE * B_HD), lambda i: (i, 0, 0)),
                  pl.BlockSpec((1, 1, kd), lambda i: (i // B_KV, 0, 0)),
                  pl.BlockSpec((1, kd, CMP_HIDDEN), lambda i: (i // B_KV, 0, 0)),
                  pl.BlockSpec((1, CMP_HIDDEN, B_HD), lambda i: (i // B_KV, 0, 0))],
        out_specs=pl.BlockSpec((1, n16, B_HD), lambda i: (i, 0, 0)),
        out_shape=jax.ShapeDtypeStruct((2 * B_KV, n16, B_HD), F32),
        compiler_params=_cparams(("parallel",)),
        name="nsa_compress",
    )(a4, pe, w1, w2)
    kc = jnp.concatenate([out[0], out[1]], axis=-1).astype(BF16)
    vc = jnp.concatenate([out[2], out[3]], axis=-1).astype(BF16)
    return kc, vc


SEL_KT = 512
SEL_KT_SHIFT = 9
WIN_KEYS = WIN + BLK


def _lanes(col):
    return jnp.broadcast_to(col, (col.shape[0], 2 * B_HD))


def _tile_lanes(rep, n):
    return jnp.concatenate([rep] * (n // (2 * B_HD)), axis=1)


def _nsa_kernel(slopes_ref, q0_ref, q1_ref, gate_ref, kc_ref, vc_ref, ks_ref, vs_ref, kw_ref, vw_ref,
                ovt_ref, ex_ref, o_ref, qg_ref, oc_ref, ow_ref, acc_ref, m_ref, a_ref, z_ref, p_ref, ps_ref):
    b = pl.program_id(0)
    t0 = b * BLK
    rows = B_REP * BLK
    ncmp = kc_ref.shape[0]
    nslc = ovt_ref.shape[0]
    ntile_all = ex_ref.shape[0]
    lane = lax.broadcasted_iota(jnp.int32, (1, 2 * B_HD), 1)
    lo = lane < B_HD
    tq = t0 + lax.broadcasted_iota(jnp.int32, (BLK, 1), 0)
    tq_lane = t0 + lax.broadcasted_iota(jnp.int32, (1, BLK), 1)
    qscale = jnp.asarray(B_HD ** -0.5, BF16)
    qnat = [q0_ref[...] * qscale, q1_ref[...] * qscale]
    gates = _sigmoid(gate_ref[...].astype(F32))
    zero = jnp.zeros((), BF16)
    one = jnp.ones((), BF16)

    cend = lax.broadcasted_iota(jnp.int32, (1, ncmp), 1) * CMP_STRIDE + (CMP_LEN - 1)
    ok_c = cend <= tq
    dist_c = (t0 + BLK - 1 - cend).astype(F32)
    halves = [lo, jnp.logical_not(lo)]
    slopes = [[slopes_ref[g * B_REP + i] for i in range(B_REP)] for g in range(B_KV)]

    for g in range(B_KV):
        slope = slopes[g]
        for i in range(B_REP):
            x = qnat[g][:, (i // 2) * 2 * B_HD:(i // 2 + 1) * 2 * B_HD]
            if i % 2 != g:
                x = jnp.concatenate([x[:, B_HD:], x[:, :B_HD]], axis=1)
            qg_ref[g, i * BLK:(i + 1) * BLK, :] = jnp.where(halves[g], x, zero)
        s = lax.dot_general(qg_ref[g], kc_ref[...], NT_DIMS, preferred_element_type=F32)
        for i in range(B_REP):
            rs = slice(i * BLK, (i + 1) * BLK)
            z = jnp.where(ok_c, s[rs] - slope[i] * dist_c, NEG)
            z_ref[rs, :ncmp] = z
            m_ref[rs, :] = _lanes(jnp.max(z, axis=-1, keepdims=True))

        @pl.when(b >= 0)
        def _():
            for i in range(B_REP):
                rs = slice(i * BLK, (i + 1) * BLK)
                e = jnp.where(ok_c, jnp.exp(z_ref[rs, :ncmp] - _tile_lanes(m_ref[rs, :], ncmp)), 0.0)
                z_ref[rs, :ncmp] = e
                a_ref[rs, :] = _lanes(jnp.maximum(jnp.sum(e, axis=-1, keepdims=True), TINY))

        @pl.when(b >= 0)
        def _():
            psum = jnp.zeros((BLK, ncmp), F32)
            for i in range(B_REP):
                rs = slice(i * BLK, (i + 1) * BLK)
                p = z_ref[rs, :ncmp] / _tile_lanes(a_ref[rs, :], ncmp)
                psum = psum + p
                p_ref[rs, :ncmp] = p.astype(BF16)
            oc_ref[g] = jnp.dot(p_ref[:, :ncmp], vc_ref[...], preferred_element_type=F32)
            ps_ref[g] = psum

    imps = []
    for g in range(B_KV):
        psum = ps_ref[g]
        p_hi = psum.astype(BF16)
        p_lo = (psum - p_hi.astype(F32)).astype(BF16)
        imps.append(lax.dot_general(ovt_ref[...], p_hi, NT_DIMS, preferred_element_type=F32)
                    + lax.dot_general(ovt_ref[...], p_lo, NT_DIMS, preferred_element_type=F32))
    imp = jnp.concatenate(imps, axis=1)
    jj = lax.broadcasted_iota(jnp.int32, (nslc, B_KV * BLK), 0)
    jf = jj.astype(F32)
    cur = jnp.right_shift(jnp.concatenate([tq_lane] * B_KV, axis=1), SLC_SHIFT)
    forced = (jj == 0) | (jj == cur) | (jj == cur - 1)
    score = jnp.where(forced, FORCE, jnp.where(jj <= cur, imp, NEG))
    sel_t = jnp.zeros((nslc, B_KV * BLK), F32)
    for _ in range(min(N_SEL, nslc)):
        mx = jnp.max(score, axis=0, keepdims=True)
        idx = jnp.min(jnp.where(score == mx, jf, 1e9), axis=0, keepdims=True)
        pick = jf == idx
        sel_t = jnp.where(pick & (mx > NEG / 2), 1.0, sel_t)
        score = jnp.where(pick, REMOVED, score)
    ntp = max(8, ntile_all)
    tile_of = jnp.right_shift(lax.broadcasted_iota(jnp.int32, (ntp, nslc), 1), SEL_KT_SHIFT - SLC_SHIFT)
    krow = lax.broadcasted_iota(jnp.int32, (ntp, nslc), 0)
    gmat = jnp.where(tile_of == krow, 1.0, 0.0).astype(BF16)
    cnt = jnp.dot(gmat, sel_t.astype(BF16), preferred_element_type=F32)
    pow2 = jnp.left_shift(1, lax.broadcasted_iota(jnp.int32, (ntp, 1), 0)).astype(F32)
    sel_bs, bitss = [], []
    for g in range(B_KV):
        gl = slice(g * BLK, (g + 1) * BLK)
        sel_bs.append(sel_t[:, gl].T.astype(BF16))
        anyq = jnp.max(cnt[:, gl], axis=-1, keepdims=True)
        bitss.append(jnp.sum(jnp.where(anyq > 0.5, pow2, 0.0)).astype(jnp.int32))

    for g in range(B_KV):
        half, slope, sel_b, bits = halves[g], slopes[g], sel_bs[g], bitss[g]

        wstart = pl.multiple_of(jnp.maximum(t0 - WIN, 0), BLK)
        uw = wstart + lax.broadcasted_iota(jnp.int32, (1, WIN_KEYS), 1)
        dw = tq - uw
        ok_w = (dw >= 0) & (dw < WIN)
        dist_w = (t0 + BLK - 1 - uw).astype(F32)
        s = lax.dot_general(qg_ref[g], kw_ref[pl.ds(wstart, WIN_KEYS), :], NT_DIMS, preferred_element_type=F32)
        for i in range(B_REP):
            rs = slice(i * BLK, (i + 1) * BLK)
            z = jnp.where(ok_w, s[rs] - slope[i] * dist_w, NEG)
            z_ref[rs, :] = z
            m_ref[rs, :] = _lanes(jnp.max(z, axis=-1, keepdims=True))

        @pl.when(b >= 0)
        def _():
            for i in range(B_REP):
                rs = slice(i * BLK, (i + 1) * BLK)
                p_ref[rs, :] = jnp.exp(z_ref[rs, :] - _tile_lanes(m_ref[rs, :], WIN_KEYS)).astype(BF16)
            vw = jnp.where(half, vw_ref[pl.ds(wstart, WIN_KEYS), :], one)
            ow = jnp.dot(p_ref[...], vw, preferred_element_type=F32)
            ow_ref[...] = ow / pltpu.roll(ow, B_HD, axis=1)

        m_ref[...] = jnp.full((rows, 2 * B_HD), NEG, F32)
        acc_ref[...] = jnp.zeros((rows, 2 * B_HD), F32)
        ntiles = (t0 + BLK + SEL_KT - 1) // SEL_KT

        def tile_body(it, carry):
            kt = ntiles - 1 - it

            @pl.when(jnp.bitwise_and(jnp.right_shift(bits, kt), 1) == 1)
            def _():
                k0 = pl.multiple_of(kt * SEL_KT, SEL_KT)
                us = k0 + lax.broadcasted_iota(jnp.int32, (1, SEL_KT), 1)
                selx = jnp.dot(sel_b, ex_ref[kt], preferred_element_type=F32)
                ok_s = (selx > 0.5) & (us <= tq)
                dist_s = (t0 + BLK - 1 - us).astype(F32)
                s = lax.dot_general(qg_ref[g], ks_ref[pl.ds(k0, SEL_KT), :], NT_DIMS, preferred_element_type=F32)
                for i in range(B_REP):
                    rs = slice(i * BLK, (i + 1) * BLK)
                    z = jnp.where(ok_s, s[rs] - slope[i] * dist_s, NEG)
                    z_ref[rs, :SEL_KT] = z
                    m_old = m_ref[rs, :]
                    m_new = jnp.maximum(m_old, _lanes(jnp.max(z, axis=-1, keepdims=True)))
                    a_ref[rs, :] = jnp.exp(m_old - m_new)
                    m_ref[rs, :] = m_new

                @pl.when(kt >= 0)
                def _():
                    for i in range(B_REP):
                        rs = slice(i * BLK, (i + 1) * BLK)
                        p_ref[rs, :SEL_KT] = jnp.exp(z_ref[rs, :SEL_KT]
                                                     - _tile_lanes(m_ref[rs, :], SEL_KT)).astype(BF16)
                    vs = jnp.where(half, vs_ref[pl.ds(k0, SEL_KT), :], one)
                    acc_ref[...] = a_ref[...] * acc_ref[...] + jnp.dot(p_ref[:, :SEL_KT], vs,
                                                                       preferred_element_type=F32)

            return carry

        lax.fori_loop(0, ntiles, tile_body, 0)

        for i in range(B_REP):
            rs = slice(i * BLK, (i + 1) * BLK)
            h = g * B_REP + i
            acc = acc_ref[rs, :]
            o_s = acc / pltpu.roll(acc, B_HD, axis=1)
            out = (gates[:, 3 * h:3 * h + 1] * oc_ref[g, rs, :] + gates[:, 3 * h + 1:3 * h + 2] * o_s
                   + gates[:, 3 * h + 2:3 * h + 3] * ow_ref[rs, :])
            cs = slice(i * 2 * B_HD, (i + 1) * 2 * B_HD)
            if g == 0:
                o_ref[:, cs] = out.astype(o_ref.dtype)
            else:
                o_ref[:, cs] = jnp.where(lo, o_ref[:, cs], out.astype(o_ref.dtype))


def _nsa_attention(proj, kc, vc, slopes):
    t = proj.shape[0]
    assert t % SEL_KT == 0 and SLC_LEN == 1 << SLC_SHIFT and SEL_KT == 1 << SEL_KT_SHIFT
    ncmp = kc.shape[0]
    nslc = t // SLC_LEN
    ntile = t // SEL_KT
    assert ntile <= 24
    c_start = jnp.arange(ncmp)[None, :] * CMP_STRIDE
    s_start = jnp.arange(nslc)[:, None] * SLC_LEN
    overlap = jnp.clip(jnp.minimum(c_start + CMP_LEN, s_start + SLC_LEN) - jnp.maximum(c_start, s_start),
                       0, None).astype(F32) / CMP_LEN
    overlap = jnp.where(jnp.arange(ncmp)[None, :] < ncmp - 1, overlap, 0.0).astype(BF16)
    key_blk = (jnp.arange(ntile)[:, None, None] * SEL_KT + jnp.arange(SEL_KT)[None, None, :]) // SLC_LEN
    expand = (key_blk == jnp.arange(nslc)[None, :, None]).astype(BF16)
    w2 = 2 * B_HD
    qcol = COL_BQ // (4 * w2)
    kvcol = COL_BKV // w2
    full = lambda c: pl.BlockSpec((t, w2), lambda b: (0, c))
    rows = B_REP * BLK
    return pl.pallas_call(
        _nsa_kernel,
        grid=(t // BLK,),
        in_specs=[pl.BlockSpec(memory_space=pltpu.SMEM),
                  pl.BlockSpec((BLK, 4 * w2), lambda b: (b, qcol)),
                  pl.BlockSpec((BLK, 4 * w2), lambda b: (b, qcol + 1)),
                  pl.BlockSpec((BLK, w2), lambda b: (b, COL_BG // w2)),
                  pl.BlockSpec((ncmp, w2), lambda b: (0, 0)),
                  pl.BlockSpec((ncmp, w2), lambda b: (0, 0)),
                  full(kvcol + 2), full(kvcol + 3), full(kvcol + 4), full(kvcol + 5),
                  pl.BlockSpec((nslc, ncmp), lambda b: (0, 0)),
                  pl.BlockSpec((ntile, nslc, SEL_KT), lambda b: (0, 0, 0))],
        out_specs=pl.BlockSpec((BLK, B_HEADS * B_HD), lambda b: (b, 0)),
        out_shape=jax.ShapeDtypeStruct((t, B_HEADS * B_HD), BF16),
        scratch_shapes=[pltpu.VMEM((B_KV, rows, w2), BF16),
                        pltpu.VMEM((B_KV, rows, w2), F32),
                        pltpu.VMEM((rows, w2), F32),
                        pltpu.VMEM((rows, w2), F32),
                        pltpu.VMEM((rows, w2), F32),
                        pltpu.VMEM((rows, w2), F32),
                        pltpu.VMEM((rows, WIN_KEYS), F32),
                        pltpu.VMEM((rows, WIN_KEYS), BF16),
                        pltpu.VMEM((B_KV, BLK, ncmp), F32)],
        compiler_params=_cparams(("parallel",)),
        name="nsa_attention",
    )(slopes, proj, proj, proj, kc, vc, proj, proj, proj, proj, overlap, expand)


MERGE_KC = 512


def _merge_kernel(o0_ref, o1_ref, o2_ref, l0_ref, l1_ref, l2_ref, yb_ref, x_ref, gn_ref, wg0_ref, wg1_ref,
                  wa_ref, wb_ref, wo_ref, h_ref, ya_ref, xn_ref):
    kc = pl.program_id(1)

    @pl.when(kc == 0)
    def _():
        l0, l1, l2 = l0_ref[...], l1_ref[...], l2_ref[...]
        m = jnp.maximum(jnp.maximum(l0, l1), l2)
        e0, e1, e2 = jnp.exp(l0 - m), jnp.exp(l1 - m), jnp.exp(l2 - m)
        den = e0 + e1 + e2
        ya = (e0 / den) * o0_ref[...] + (e1 / den) * o1_ref[...] + (e2 / den) * o2_ref[...]
        ya_ref[...] = ya.astype(BF16)
        x = x_ref[...]
        h_ref[...] = x
        xn_ref[...] = _rms(x, gn_ref[...]).astype(BF16)

    g0 = _sigmoid(jnp.dot(xn_ref[...], wg0_ref[...], preferred_element_type=F32))
    g1 = _sigmoid(jnp.dot(xn_ref[...], wg1_ref[...], preferred_element_type=F32))
    ua = jnp.dot(ya_ref[...], wa_ref[...], preferred_element_type=F32)
    ub = jnp.dot(yb_ref[...], wb_ref[...], preferred_element_type=F32)
    mix = g0 * ua + g1 * ub
    h_ref[...] += jnp.dot(mix.astype(BF16), wo_ref[...], preferred_element_type=F32)


def _merge(x, norm_mix, w_gate, dil_o, dil_lse, yb, wa, wb, wo, tm):
    t, d = x.shape
    nk = d // MERGE_KC
    row = lambda w: pl.BlockSpec((tm, w), lambda i, k: (i, 0))
    return pl.pallas_call(
        _merge_kernel,
        grid=(t // tm, nk),
        in_specs=[row(A_OUT)] * 6 + [row(B_HEADS * B_HD), row(d),
                  pl.BlockSpec((1, d), lambda i, k: (0, 0)),
                  pl.BlockSpec((d, MERGE_KC), lambda i, k: (0, k)),
                  pl.BlockSpec((d, MERGE_KC), lambda i, k: (0, nk + k)),
                  pl.BlockSpec((A_OUT, MERGE_KC), lambda i, k: (0, k)),
                  pl.BlockSpec((B_HEADS * B_HD, MERGE_KC), lambda i, k: (0, k)),
                  pl.BlockSpec((MERGE_KC, d), lambda i, k: (k, 0))],
        out_specs=pl.BlockSpec((tm, d), lambda i, k: (i, 0)),
        out_shape=jax.ShapeDtypeStruct((t, d), F32),
        scratch_shapes=[pltpu.VMEM((tm, A_OUT), BF16), pltpu.VMEM((tm, d), BF16)],
        compiler_params=_cparams(("parallel", "arbitrary")),
        name="merge",
    )(*dil_o, *dil_lse, yb, x, norm_mix.reshape(1, d), w_gate, w_gate, wa, wb, wo)


def _cross_kernel(h_ref, gx_ref, gf_ref, wq_ref, kv_ref, wo_ref, h2_ref, xn_ref):
    h = h_ref[...]
    hn = _rms(h, gx_ref[...]).astype(BF16)
    q = jnp.dot(hn, wq_ref[...], preferred_element_type=F32).astype(BF16)
    kv = kv_ref[...]
    nh = X_HEADS * X_HD
    outs = []
    for hd in range(X_HEADS):
        sl = slice(hd * X_HD, (hd + 1) * X_HD)
        s = lax.dot_general(q[:, sl], kv[:, sl], NT_DIMS, preferred_element_type=F32) * (X_HD ** -0.5)
        m = jnp.max(s, axis=-1, keepdims=True)
        p = jnp.exp(s - m)
        p = p / jnp.sum(p, axis=-1, keepdims=True)
        outs.append(jnp.dot(p.astype(BF16), kv[:, nh + hd * X_HD:nh + (hd + 1) * X_HD],
                            preferred_element_type=F32))
    o = jnp.concatenate(outs, axis=1).astype(BF16)
    h2 = h + jnp.dot(o, wo_ref[...], preferred_element_type=F32)
    h2_ref[...] = h2
    xn_ref[...] = _rms(h2, gf_ref[...]).astype(BF16)


def _cross(h1, norm_x, norm_ffn, wq, kv, wo, tm):
    t, d = h1.shape
    nm = kv.shape[0]
    nh = X_HEADS * X_HD
    const = lambda shape: pl.BlockSpec(shape, lambda i: (0, 0))
    return pl.pallas_call(
        _cross_kernel,
        grid=(t // tm,),
        in_specs=[pl.BlockSpec((tm, d), lambda i: (i, 0)), const((1, d)), const((1, d)),
                  const((d, nh)), const((nm, 2 * nh)), const((nh, d))],
        out_specs=[pl.BlockSpec((tm, d), lambda i: (i, 0)), pl.BlockSpec((tm, d), lambda i: (i, 0))],
        out_shape=[jax.ShapeDtypeStruct((t, d), F32), jax.ShapeDtypeStruct((t, d), BF16)],
        compiler_params=_cparams(("parallel",)),
        name="cross_attn",
    )(h1, norm_x.reshape(1, d), norm_ffn.reshape(1, d), wq, kv, wo)


ROUTE_TB = 128
ROUTE_HPI = 4
MAP_TOK = 16


def _top16_rows(s, rowf, nrows):
    vals, idxs = [], []
    for _ in range(PEER_TOPK):
        mx = jnp.max(s, axis=0, keepdims=True)
        idx = jnp.min(jnp.where(s == mx, rowf, float(nrows)), axis=0, keepdims=True)
        vals.append(mx)
        idxs.append(idx)
        s = jnp.where(rowf == idx, REMOVED, s)
    return vals, idxs


def _route_kernel(xn_ref, wq_ref, k1_ref, k2_ref, w_ref, q_scr, g_scr, e1_scr, e2_scr, gt_scr, e1t_scr, e2t_scr,
                  wt_scr):
    tb = xn_ref.shape[0]
    half = PEER_DK // 2
    q = jnp.dot(xn_ref[...], wq_ref[...], preferred_element_type=F32).astype(BF16)
    for c in range(2 * PEER_HEADS):
        q_scr[c] = q[:, c * half:(c + 1) * half]
    nl = ROUTE_HPI * tb
    rowf = lax.broadcasted_iota(jnp.int32, (N_KEYS, nl), 0).astype(F32)
    nc = 8 * 10
    rowc = lax.broadcasted_iota(jnp.int32, (nc, nl), 0).astype(F32)

    def head_body(hp, carry):
        h0 = hp * ROUTE_HPI
        s1 = jnp.concatenate([lax.dot_general(k1_ref[...], q_scr[2 * (h0 + u)], NT_DIMS, preferred_element_type=F32)
                              for u in range(ROUTE_HPI)], axis=1)
        s2 = jnp.concatenate([lax.dot_general(k2_ref[...], q_scr[2 * (h0 + u) + 1], NT_DIMS,
                                              preferred_element_type=F32) for u in range(ROUTE_HPI)], axis=1)
        v1, i1 = _top16_rows(s1, rowf, N_KEYS)
        v2, i2 = _top16_rows(s2, rowf, N_KEYS)
        sub8 = lax.broadcasted_iota(jnp.int32, (8, nl), 0)
        v2lo, v2hi = jnp.concatenate(v2[:8], axis=0), jnp.concatenate(v2[8:], axis=0)
        i2lo, i2hi = jnp.concatenate(i2[:8], axis=0), jnp.concatenate(i2[8:], axis=0)
        cand_p = [v1[0] + v2lo, v1[0] + v2hi]
        e1_p = [jnp.broadcast_to(i1[0], (8, nl))] * 2
        e2_p = [i2lo, i2hi]
        for a in range(1, 8):
            nb = PEER_TOPK // (a + 1)
            piece = v1[a] + v2lo
            cand_p.append(piece if nb >= 8 else jnp.where(sub8 < nb, piece, REMOVED))
            e1_p.append(jnp.broadcast_to(i1[a], (8, nl)))
            e2_p.append(i2lo)
        cand_p.append(jnp.concatenate(v1[8:], axis=0) + v2[0])
        e1_p.append(jnp.concatenate(i1[8:], axis=0))
        e2_p.append(jnp.broadcast_to(i2[0], (8, nl)))
        cand = jnp.concatenate(cand_p, axis=0)
        e1c = jnp.concatenate(e1_p, axis=0)
        e2c = jnp.concatenate(e2_p, axis=0)
        vs, e1, e2 = [], [], []
        for _ in range(PEER_TOPK):
            mx = jnp.max(cand, axis=0, keepdims=True)
            idx = jnp.min(jnp.where(cand == mx, rowc, float(nc)), axis=0, keepdims=True)
            pick = rowc == idx
            vs.append(mx)
            e1.append(jnp.sum(jnp.where(pick, e1c, 0.0), axis=0, keepdims=True))
            e2.append(jnp.sum(jnp.where(pick, e2c, 0.0), axis=0, keepdims=True))
            cand = jnp.where(pick, REMOVED, cand)
        vsa = jnp.concatenate(vs, axis=0)
        ex = jnp.exp(vsa - vs[0])
        gate = ex / jnp.sum(ex, axis=0, keepdims=True)
        e1a, e2a = jnp.concatenate(e1, axis=0), jnp.concatenate(e2, axis=0)
        for u in range(ROUTE_HPI):
            r0 = pl.multiple_of((h0 + u) * PEER_TOPK, PEER_TOPK)
            ls = slice(u * tb, (u + 1) * tb)
            g_scr[pl.ds(r0, PEER_TOPK), :] = gate[:, ls]
            e1_scr[pl.ds(r0, PEER_TOPK), :] = e1a[:, ls]
            e2_scr[pl.ds(r0, PEER_TOPK), :] = e2a[:, ls]
        return carry

    lax.fori_loop(0, PEER_HEADS // ROUTE_HPI, head_body, 0)

    gt_scr[...] = g_scr[...].T
    e1t_scr[...] = e1_scr[...].T
    e2t_scr[...] = e2_scr[...].T
    sub = lax.broadcasted_iota(jnp.int32, (N_KEYS, PEER_HEADS * PEER_TOPK), 0).astype(F32)

    def group_body(grp, carry):
        shape = (N_KEYS, PEER_HEADS * PEER_TOPK)
        for tt in range(MAP_TOK):
            t = grp * MAP_TOK + tt
            g_row = jnp.broadcast_to(gt_scr[pl.ds(t, 1), :], shape)
            e1_row = jnp.broadcast_to(e1t_scr[pl.ds(t, 1), :], shape)
            e2_row = jnp.broadcast_to(e2t_scr[pl.ds(t, 1), :], shape)
            a = jnp.where(e1_row == sub, g_row, 0.0).astype(BF16)
            bm = jnp.where(e2_row == sub, 1.0, 0.0).astype(BF16)
            wt_scr[pl.ds(tt, N_KEYS, stride=MAP_TOK), :] = lax.dot_general(a, bm, NT_DIMS,
                                                                          preferred_element_type=F32)
        w_ref[grp] = wt_scr[...].astype(w_ref.dtype)
        return carry

    lax.fori_loop(0, tb // MAP_TOK, group_body, 0)


def _peer_route(xn, wq, k1, k2):
    t, d = xn.shape
    tb = ROUTE_TB
    half = PEER_DK // 2
    slots = PEER_HEADS * PEER_TOPK
    const = lambda shape: pl.BlockSpec(shape, lambda i: (0,) * len(shape))
    return pl.pallas_call(
        _route_kernel,
        grid=(t // tb,),
        in_specs=[pl.BlockSpec((tb, d), lambda i: (i, 0)), const((d, PEER_HEADS * PEER_DK)),
                  const((N_KEYS, half)), const((N_KEYS, half))],
        out_specs=pl.BlockSpec((tb // MAP_TOK, N_KEYS * MAP_TOK, N_KEYS), lambda i: (i, 0, 0)),
        out_shape=jax.ShapeDtypeStruct((t // MAP_TOK, N_KEYS * MAP_TOK, N_KEYS), BF16),
        scratch_shapes=[pltpu.VMEM((2 * PEER_HEADS, tb, half), BF16),
                        pltpu.VMEM((slots, tb), F32), pltpu.VMEM((slots, tb), F32), pltpu.VMEM((slots, tb), F32),
                        pltpu.VMEM((tb, slots), F32), pltpu.VMEM((tb, slots), F32), pltpu.VMEM((tb, slots), F32),
                        pltpu.VMEM((N_KEYS * MAP_TOK, N_KEYS), F32)],
        compiler_params=_cparams(("parallel",)),
        name="peer_route",
    )(xn, wq, k1, k2)


def _castT_kernel(u_ref, o_ref):
    o_ref[...] = u_ref[...].T.astype(o_ref.dtype)


def _cast_transpose(u, te, td):
    ne, d = u.shape
    return pl.pallas_call(
        _castT_kernel,
        grid=(ne // te, d // td),
        in_specs=[pl.BlockSpec((te, td), lambda i, j: (i, j))],
        out_specs=pl.BlockSpec((td, te), lambda i, j: (j, i)),
        out_shape=jax.ShapeDtypeStruct((d, ne), BF16),
        compiler_params=_cparams(("parallel", "parallel")),
        name="expert_u_transpose",
    )(u)


def _dense_kernel(xn_ref, ut_ref, w_ref, v_ref, h_ref, gf_ref, o_ref, acc_ref, aw_ref):
    e = pl.program_id(1)

    @pl.when(e == 0)
    def _():
        acc_ref[...] = jnp.zeros_like(acc_ref)

    tb, eb = aw_ref.shape
    a = _gelu(jnp.dot(xn_ref[...], ut_ref[...], preferred_element_type=F32))
    for bb in range(tb // MAP_TOK):
        rs = slice(bb * MAP_TOK, (bb + 1) * MAP_TOK)
        for j in range(eb // N_KEYS):
            cs = slice(j * N_KEYS, (j + 1) * N_KEYS)
            aw_ref[rs, cs] = (a[rs, cs] * w_ref[bb, j * MAP_TOK:(j + 1) * MAP_TOK, :].astype(F32)).astype(BF16)
    acc_ref[...] += jnp.dot(aw_ref[...], v_ref[...], preferred_element_type=F32)

    @pl.when(e == pl.num_programs(1) - 1)
    def _():
        o_ref[...] = _rms(h_ref[...] + acc_ref[...], gf_ref[...]).astype(o_ref.dtype)


def _peer_dense(xn, ut, wmap, v, h2, norm_final, tb, eb):
    t, d = xn.shape
    ne = ut.shape[1]
    return pl.pallas_call(
        _dense_kernel,
        grid=(t // tb, ne // eb),
        in_specs=[pl.BlockSpec((tb, d), lambda i, e: (i, 0)),
                  pl.BlockSpec((d, eb), lambda i, e: (0, e)),
                  pl.BlockSpec((tb // MAP_TOK, eb // N_KEYS * MAP_TOK, N_KEYS), lambda i, e: (i, e, 0)),
                  pl.BlockSpec((eb, d), lambda i, e: (e, 0)),
                  pl.BlockSpec((tb, d), lambda i, e: (i, 0)),
                  pl.BlockSpec((1, d), lambda i, e: (0, 0))],
        out_specs=pl.BlockSpec((tb, d), lambda i, e: (i, 0)),
        out_shape=jax.ShapeDtypeStruct((t, d), F32),
        scratch_shapes=[pltpu.VMEM((tb, d), F32), pltpu.VMEM((tb, eb), BF16)],
        compiler_params=_cparams(("parallel", "arbitrary")),
        name="peer_dense",
    )(xn, ut, wmap, v, h2, norm_final.reshape(1, d))


def _alibi_slopes(n):
    return jnp.power(2.0, -8.0 * (jnp.arange(n, dtype=F32) + 1.0) / n)


def _prep_w_in(w_in):
    split = COL_BG + B_GATE_COLS
    w_mix = jnp.pad(w_in[:, :split].astype(BF16), ((0, 0), (0, PROJ_COLS - split)))
    return w_mix, w_in[:, split:].astype(BF16)


def _block(x, mem, norm_mix, w_in, w_cmp_k1, w_cmp_k2, pe_cmp_k, w_cmp_v1, w_cmp_v2, pe_cmp_v,
           w_up_a, w_up_b, w_out, norm_x, norm_mem, w_xq, w_xkv, w_xo, norm_ffn, w_pq,
           sub_keys1, sub_keys2, expert_u, expert_v, norm_out, out_dtype):
    t, d = x.shape
    assert d == D_MODEL and w_in.shape[1] == COL_BG + B_GATE_COLS + 2 * D_MODEL
    tm = min(1024, t)

    w_mix, w_gate = _prep_w_in(w_in)
    proj = _norm_matmul(x, norm_mix, w_mix, tm, 512, BF16)

    slopes_a = _alibi_slopes(A_GROUPS * A_HEADS)
    dil = [_dilated_group(proj, slopes_a, g) for g in range(A_GROUPS)]

    kc, vc = _nsa_compress(proj, pe_cmp_k, pe_cmp_v, w_cmp_k1, w_cmp_k2, w_cmp_v1, w_cmp_v2)
    yb = _nsa_attention(proj, kc, vc, _alibi_slopes(B_HEADS))

    wb = w_up_b.reshape(B_KV, B_REP, B_HD, d).transpose(1, 0, 2, 3).reshape(B_HEADS * B_HD, d)
    h1 = _merge(x, norm_mix, w_gate, [o for o, _ in dil], [l for _, l in dil], yb,
                w_up_a.astype(BF16), wb.astype(BF16), w_out.astype(BF16), min(512, t))

    kvm = _norm_matmul(mem, norm_mem, w_xkv.astype(BF16), mem.shape[0], 512, BF16)
    h2, xn = _cross(h1, norm_x, norm_ffn, w_xq.astype(BF16), kvm, w_xo.astype(BF16), min(512, t))

    wmap = _peer_route(xn, w_pq.astype(BF16), sub_keys1.astype(BF16), sub_keys2.astype(BF16))
    return _peer_dense(xn, _cast_transpose(expert_u, 512, 1024), wmap, expert_v.astype(BF16), h2, norm_out,
                       min(512, t), 1024).astype(out_dtype)


@jax.jit
def kernel(x, mem, norm_mix, w_in, w_cmp_k1, w_cmp_k2, pe_cmp_k, w_cmp_v1, w_cmp_v2, pe_cmp_v, w_up_a, w_up_b,
           w_out, norm_x, norm_mem, w_xq, w_xkv, w_xo, norm_ffn, w_pq, sub_keys1, sub_keys2, expert_u,
           expert_v, norm_final):
    assert x.shape[0] == 1 and mem.shape[0] == 1 and norm_mix.shape[0] == 1
    out = _block(x[0], mem[0], norm_mix[0], w_in[0], w_cmp_k1[0], w_cmp_k2[0], pe_cmp_k[0], w_cmp_v1[0],
                 w_cmp_v2[0], pe_cmp_v[0], w_up_a[0], w_up_b[0], w_out[0], norm_x[0], norm_mem[0], w_xq[0],
                 w_xkv[0], w_xo[0], norm_ffn[0], w_pq[0], sub_keys1[0], sub_keys2[0], expert_u[0],
                 expert_v[0], norm_final, x.dtype)
    return out[None]
```

```python
import functools

import jax
import jax.numpy as jnp
from jax import lax
from jax.experimental import pallas as pl
from jax.experimental.pallas import tpu as pltpu

F32 = jnp.float32
BF16 = jnp.bfloat16

D_MODEL = 2048
BLK = 128
EPS = 1e-6
NEG = -1e30
TINY = 1e-30
FORCE = 1e4
REMOVED = -3e38

A_DILATIONS = (1, 4, 16)
A_WINDOWS = (128, 512, 2048)
A_GROUPS = 3
A_HEADS = 4
A_HD = 128
B_HEADS = 16
B_KV = 2
B_REP = B_HEADS // B_KV
B_HD = 64
CMP_LEN = 32
CMP_STRIDE = 16
CMP_HIDDEN = 128
SLC_LEN = 64
SLC_SHIFT = 6
N_SEL = 16
WIN = 512
X_HEADS = 4
X_HD = 128
PEER_HEADS = 8
PEER_DK = 256
N_KEYS = 128
N_EXPERTS = N_KEYS * N_KEYS
PEER_TOPK = 16

A_QKV_COLS = 3 * A_GROUPS * A_HEADS * A_HD
A_OUT = A_HEADS * A_HD
B_Q_COLS = B_HEADS * B_HD
B_KV_COLS = 3 * 2 * B_KV * B_HD
B_GATE_COLS = B_HEADS * 3

COL_A = 0
COL_BQ = COL_A + A_QKV_COLS
COL_BKV = COL_BQ + B_Q_COLS
COL_BG = COL_BKV + B_KV_COLS
PROJ_COLS = 6656

VMEM_LIMIT = 56 * 1024 * 1024

NT_DIMS = (((1,), (1,)), ((), ()))


def _cparams(sem):
    return pltpu.CompilerParams(dimension_semantics=sem, vmem_limit_bytes=VMEM_LIMIT)


def _rms(x, g):
    ms = jnp.mean(x * x, axis=-1, keepdims=True)
    return x * lax.rsqrt(ms + EPS) * g


_GELU_K1 = -2.0 * 0.7978845608028654 * 1.4426950408889634
_GELU_K2 = _GELU_K1 * 0.044715


def _gelu(x):
    return x / (1.0 + jnp.exp2(x * (_GELU_K1 + _GELU_K2 * (x * x))))


def _sigmoid(x):
    return 1.0 / (1.0 + jnp.exp(-x))


def _nmm_kernel(x_ref, g_ref, w_ref, o_ref, xn_ref):
    @pl.when(pl.program_id(1) == 0)
    def _():
        xn_ref[...] = _rms(x_ref[...].astype(F32), g_ref[...]).astype(BF16)

    o_ref[...] = jnp.dot(xn_ref[...], w_ref[...], preferred_element_type=F32).astype(o_ref.dtype)


def _norm_matmul(x, gain, w, tm, tn, out_dtype):
    t, d = x.shape
    n = w.shape[1]
    return pl.pallas_call(
        _nmm_kernel,
        grid=(t // tm, n // tn),
        in_specs=[pl.BlockSpec((tm, d), lambda i, j: (i, 0)),
                  pl.BlockSpec((1, d), lambda i, j: (0, 0)),
                  pl.BlockSpec((d, tn), lambda i, j: (0, j))],
        out_specs=pl.BlockSpec((tm, tn), lambda i, j: (i, j)),
        out_shape=jax.ShapeDtypeStruct((t, n), out_dtype),
        scratch_shapes=[pltpu.VMEM((tm, d), BF16)],
        compiler_params=_cparams(("parallel", "arbitrary")),
        name="norm_matmul",
    )(x, gain.reshape(1, d), w)


IN_TN = A_OUT


def _inproj_kernel(x_ref, g_ref, w_ref, proj_ref, *rest):
    dil_refs, (xn_ref, res_ref) = rest[:-2], rest[-2:]
    j = pl.program_id(1)
    tm = x_ref.shape[0]

    @pl.when(j == 0)
    def _():
        xn_ref[...] = _rms(x_ref[...], g_ref[...]).astype(BF16)

    res = jnp.dot(xn_ref[...], w_ref[...].astype(BF16), preferred_element_type=F32)
    proj_ref[...] = res.astype(proj_ref.dtype)
    for out_ref, group in zip(dil_refs, range(1, A_GROUPS)):
        dil = A_DILATIONS[group]
        for s in range(3):
            @pl.when(j == s * A_GROUPS + group)
            def _(out_ref=out_ref, dil=dil, s=s):
                for c in range(IN_TN // A_HD):
                    res_ref[c] = res[:, c * A_HD:(c + 1) * A_HD]
                for r in range(dil):
                    for c in range(IN_TN // A_HD):
                        c0 = (r * 3 + s) * IN_TN + c * A_HD
                        out_ref[:, c0:c0 + A_HD] = res_ref[c, pl.ds(r, tm // dil, stride=dil), :].astype(out_ref.dtype)


def _in_projection(x, gain, w_in, tm):
    t, d = x.shape
    groups = range(1, A_GROUPS)
    assert all(tm % A_DILATIONS[g] == 0 and (tm // A_DILATIONS[g]) % 16 == 0 for g in groups)
    dil_specs = [pl.BlockSpec((tm // A_DILATIONS[g], A_DILATIONS[g] * 3 * A_OUT), lambda i, j: (i, 0)) for g in groups]
    dil_shapes = [jax.ShapeDtypeStruct((t // A_DILATIONS[g], A_DILATIONS[g] * 3 * A_OUT), BF16) for g in groups]
    return pl.pallas_call(
        _inproj_kernel,
        grid=(t // tm, PROJ_COLS // IN_TN),
        in_specs=[pl.BlockSpec((tm, d), lambda i, j: (i, 0)),
                  pl.BlockSpec((1, d), lambda i, j: (0, 0)),
                  pl.BlockSpec((d, IN_TN), lambda i, j: (0, j))],
        out_specs=[pl.BlockSpec((tm, IN_TN), lambda i, j: (i, j))] + dil_specs,
        out_shape=[jax.ShapeDtypeStruct((t, PROJ_COLS), BF16)] + dil_shapes,
        scratch_shapes=[pltpu.VMEM((tm, d), BF16), pltpu.VMEM((IN_TN // A_HD, tm, A_HD), F32)],
        compiler_params=_cparams(("parallel", "arbitrary")),
        name="in_projection",
    )(x, gain.reshape(1, d), w_in)


def _dil_kernel(slopes_ref, q_ref, kp_ref, ko_ref, vp_ref, vo_ref, o_ref, lse_ref, z_ref, m_ref, *, dil, qb, group):
    b = pl.program_id(0)
    rows = pl.ds(pl.program_id(1), qb, stride=dil) if dil > 1 else slice(None)
    nk = qb + BLK
    q = q_ref[...]
    k = jnp.concatenate([kp_ref[...], ko_ref[...]], axis=0)
    i = lax.broadcasted_iota(jnp.int32, (qb, nk), 0)
    j = lax.broadcasted_iota(jnp.int32, (qb, nk), 1)
    rel = i + BLK - j
    jmin = jnp.where(b > 0, 0, BLK)
    ok = (rel >= 0) & (rel <= BLK) & (j >= jmin)
    dist = (rel * dil).astype(F32)
    scale = A_HD ** -0.5
    for h in range(A_HEADS):
        sl = slice(h * A_HD, (h + 1) * A_HD)
        s = lax.dot_general(q[:, sl], k[:, sl], NT_DIMS, preferred_element_type=F32) * scale
        s = jnp.where(ok, s - slopes_ref[group * A_HEADS + h] * dist, NEG)
        z_ref[h] = s
        m_ref[h] = jnp.broadcast_to(jnp.max(s, axis=-1, keepdims=True), (qb, A_HD))

    @pl.when(b >= 0)
    def _():
        v = jnp.concatenate([vp_ref[...], vo_ref[...]], axis=0)
        for h in range(A_HEADS):
            sl = slice(h * A_HD, (h + 1) * A_HD)
            m = m_ref[h]
            p = jnp.exp(z_ref[h] - jnp.concatenate([m] * (nk // A_HD), axis=1))
            den = jnp.sum(p, axis=-1, keepdims=True)
            o_ref[h, rows, :] = jnp.dot(p.astype(BF16), v[:, sl], preferred_element_type=F32) / den
            lse_ref[h, rows, :] = m + jnp.log(den)


def _dilated_group(src, slopes, group, t):
    dil = A_DILATIONS[group]
    assert A_WINDOWS[group] // dil == BLK
    l = t // dil
    qb = min(256 if dil < 16 else BLK, l)
    assert l % qb == 0 and qb % BLK == 0
    stride, first, step = (PROJ_COLS // A_OUT, group, A_GROUPS) if dil == 1 else (3, 0, 1)

    def col(s):
        return lambda b, r: (b, r * stride + first + s * step)

    def colp(s):
        return lambda b, r: (jnp.maximum(b * (qb // BLK) - 1, 0), r * stride + first + s * step)

    out_spec = pl.BlockSpec((A_HEADS, qb * dil, A_HD), lambda b, r: (0, b, 0))
    out_shape = jax.ShapeDtypeStruct((A_HEADS, t, A_HD), F32)
    return pl.pallas_call(
        functools.partial(_dil_kernel, dil=dil, qb=qb, group=group),
        grid=(l // qb, dil),
        in_specs=[pl.BlockSpec(memory_space=pltpu.SMEM),
                  pl.BlockSpec((qb, A_OUT), col(0)),
                  pl.BlockSpec((BLK, A_OUT), colp(1)),
                  pl.BlockSpec((qb, A_OUT), col(1)),
                  pl.BlockSpec((BLK, A_OUT), colp(2)),
                  pl.BlockSpec((qb, A_OUT), col(2))],
        out_specs=[out_spec, out_spec],
        out_shape=[out_shape, out_shape],
        scratch_shapes=[pltpu.VMEM((A_HEADS, qb, qb + BLK), F32), pltpu.VMEM((A_HEADS, qb, A_HD), F32)],
        compiler_params=_cparams(("parallel", "arbitrary")),
        name=f"dilated_attn_g{group}",
    )(slopes, src, src, src, src, src)


def _cmp_kernel(a_ref, pe_ref, w1_ref, w2_ref, o_ref):
    a = a_ref[0]
    n16 = a.shape[0]
    nxt = pltpu.roll(a, n16 - 1, axis=0)
    blocks = jnp.concatenate([a, nxt], axis=1) + pe_ref[0]
    hid = _gelu(jnp.dot(blocks.astype(BF16), w1_ref[0], preferred_element_type=F32))
    out = jnp.dot(hid.astype(BF16), w2_ref[0], preferred_element_type=F32)
    row = lax.broadcasted_iota(jnp.int32, out.shape, 0)
    o_ref[0] = jnp.where(row < n16 - 1, out, 0.0)


def _nsa_compress(proj, pe_k, pe_v, w_k1, w_k2, w_v1, w_v2):
    t = proj.shape[0]
    assert CMP_LEN == 2 * CMP_STRIDE and t % CMP_STRIDE == 0
    n16 = t // CMP_STRIDE
    kv = proj[:, COL_BKV:COL_BKV + 2 * B_KV * B_HD].astype(F32)
    a4 = kv.reshape(t, 2 * B_KV, B_HD).transpose(1, 0, 2).reshape(2 * B_KV, n16, CMP_STRIDE * B_HD)
    pe = jnp.stack([pe_k.reshape(1, -1), pe_v.reshape(1, -1)])
    w1 = jnp.stack([w_k1, w_v1]).astype(BF16)
    w2 = jnp.stack([w_k2, w_v2]).astype(BF16)
    kd = CMP_LEN * B_HD
    out = pl.pallas_call(
        _cmp_kernel,
        grid=(2 * B_KV,),
        in_specs=[pl.BlockSpec((1, n16, CMP_STRIDE * B_HD), lambda i: (i, 0, 0)),
                  pl.BlockSpec((1, 1, kd), lambda i: (i // B_KV, 0, 0)),
                  pl.BlockSpec((1, kd, CMP_HIDDEN), lambda i: (i // B_KV, 0, 0)),
                  pl.BlockSpec((1, CMP_HIDDEN, B_HD), lambda i: (i // B_KV, 0, 0))],
        out_specs=pl.BlockSpec((1, n16, B_HD), lambda i: (i, 0, 0)),
        out_shape=jax.ShapeDtypeStruct((2 * B_KV, n16, B_HD), F32),
        compiler_params=_cparams(("parallel",)),
        name="nsa_compress",
    )(a4, pe, w1, w2)
    kc = jnp.concatenate([out[0], out[1]], axis=-1).astype(BF16)
    vc = jnp.concatenate([out[2], out[3]], axis=-1).astype(BF16)
    return kc, vc


SEL_KT = 512
SEL_KT_SHIFT = 9
WIN_KEYS = WIN + BLK


def _lanes(col):
    return jnp.broadcast_to(col, (col.shape[0], 2 * B_HD))


def _tile_lanes(rep, n):
    return jnp.concatenate([rep] * (n // (2 * B_HD)), axis=1)


def _nsa_kernel(slopes_ref, q0_ref, q1_ref, gate_ref, kc_ref, vc_ref, ks_ref, vs_ref, kw_ref, vw_ref,
                ovt_ref, ex_ref, o_ref, qg_ref, oc_ref, ow_ref, acc_ref, m_ref, a_ref, z_ref, p_ref, ps_ref):
    b = pl.program_id(0)
    t0 = b * BLK
    rows = B_REP * BLK
    ncmp = kc_ref.shape[0]
    nslc = ovt_ref.shape[0]
    ntile_all = ex_ref.shape[0]
    lane = lax.broadcasted_iota(jnp.int32, (1, 2 * B_HD), 1)
    lo = lane < B_HD
    tq = t0 + lax.broadcasted_iota(jnp.int32, (BLK, 1), 0)
    tq_lane = t0 + lax.broadcasted_iota(jnp.int32, (1, BLK), 1)
    qscale = jnp.asarray(B_HD ** -0.5, BF16)
    qnat = [q0_ref[...] * qscale, q1_ref[...] * qscale]
    gates = _sigmoid(gate_ref[...].astype(F32))
    zero = jnp.zeros((), BF16)
    one = jnp.ones((), BF16)

    cend = lax.broadcasted_iota(jnp.int32, (1, ncmp), 1) * CMP_STRIDE + (CMP_LEN - 1)
    ok_c = cend <= tq
    dist_c = (t0 + BLK - 1 - cend).astype(F32)
    halves = [lo, jnp.logical_not(lo)]
    slopes = [[slopes_ref[g * B_REP + i] for i in range(B_REP)] for g in range(B_KV)]

    for g in range(B_KV):
        slope = slopes[g]
        for i in range(B_REP):
            x = qnat[g][:, (i // 2) * 2 * B_HD:(i // 2 + 1) * 2 * B_HD]
            if i % 2 != g:
                x = jnp.concatenate([x[:, B_HD:], x[:, :B_HD]], axis=1)
            qg_ref[g, i * BLK:(i + 1) * BLK, :] = jnp.where(halves[g], x, zero)
        s = lax.dot_general(qg_ref[g], kc_ref[...], NT_DIMS, preferred_element_type=F32)
        for i in range(B_REP):
            rs = slice(i * BLK, (i + 1) * BLK)
            z = jnp.where(ok_c, s[rs] - slope[i] * dist_c, NEG)
            z_ref[rs, :ncmp] = z
            m_ref[rs, :] = _lanes(jnp.max(z, axis=-1, keepdims=True))

        @pl.when(b >= 0)
        def _():
            for i in range(B_REP):
                rs = slice(i * BLK, (i + 1) * BLK)
                e = jnp.where(ok_c, jnp.exp(z_ref[rs, :ncmp] - _tile_lanes(m_ref[rs, :], ncmp)), 0.0)
                z_ref[rs, :ncmp] = e
                a_ref[rs, :] = _lanes(jnp.maximum(jnp.sum(e, axis=-1, keepdims=True), TINY))

        @pl.when(b >= 0)
        def _():
            psum = jnp.zeros((BLK, ncmp), F32)
            for i in range(B_REP):
                rs = slice(i * BLK, (i + 1) * BLK)
                p = z_ref[rs, :ncmp] / _tile_lanes(a_ref[rs, :], ncmp)
                psum = psum + p
                p_ref[rs, :ncmp] = p.astype(BF16)
            oc_ref[g] = jnp.dot(p_ref[:, :ncmp], vc_ref[...], preferred_element_type=F32)
            ps_ref[g] = psum

    imps = []
    for g in range(B_KV):
        psum = ps_ref[g]
        p_hi = psum.astype(BF16)
        p_lo = (psum - p_hi.astype(F32)).astype(BF16)
        imps.append(lax.dot_general(ovt_ref[...], p_hi, NT_DIMS, preferred_element_type=F32)
                    + lax.dot_general(ovt_ref[...], p_lo, NT_DIMS, preferred_element_type=F32))
    imp = jnp.concatenate(imps, axis=1)
    jj = lax.broadcasted_iota(jnp.int32, (nslc, B_KV * BLK), 0)
    jf = jj.astype(F32)
    cur = jnp.right_shift(jnp.concatenate([tq_lane] * B_KV, axis=1), SLC_SHIFT)
    forced = (jj == 0) | (jj == cur) | (jj == cur - 1)
    score = jnp.where(forced, FORCE, jnp.where(jj <= cur, imp, NEG))
    sel_t = jnp.zeros((nslc, B_KV * BLK), F32)
    for _ in range(min(N_SEL, nslc)):
        mx = jnp.max(score, axis=0, keepdims=True)
        idx = jnp.min(jnp.where(score == mx, jf, 1e9), axis=0, keepdims=True)
        pick = jf == idx
        sel_t = jnp.where(pick & (mx > NEG / 2), 1.0, sel_t)
        score = jnp.where(pick, REMOVED, score)
    ntp = max(8, ntile_all)
    tile_of = jnp.right_shift(lax.broadcasted_iota(jnp.int32, (ntp, nslc), 1), SEL_KT_SHIFT - SLC_SHIFT)
    krow = lax.broadcasted_iota(jnp.int32, (ntp, nslc), 0)
    gmat = jnp.where(tile_of == krow, 1.0, 0.0).astype(BF16)
    cnt = jnp.dot(gmat, sel_t.astype(BF16), preferred_element_type=F32)
    pow2 = jnp.left_shift(1, lax.broadcasted_iota(jnp.int32, (ntp, 1), 0)).astype(F32)
    sel_bs, bitss = [], []
    for g in range(B_KV):
        gl = slice(g * BLK, (g + 1) * BLK)
        sel_bs.append(sel_t[:, gl].T.astype(BF16))
        anyq = jnp.max(cnt[:, gl], axis=-1, keepdims=True)
        bitss.append(jnp.sum(jnp.where(anyq > 0.5, pow2, 0.0)).astype(jnp.int32))

    for g in range(B_KV):
        half, slope, sel_b, bits = halves[g], slopes[g], sel_bs[g], bitss[g]

        wstart = pl.multiple_of(jnp.maximum(t0 - WIN, 0), BLK)
        uw = wstart + lax.broadcasted_iota(jnp.int32, (1, WIN_KEYS), 1)
        dw = tq - uw
        ok_w = (dw >= 0) & (dw < WIN)
        dist_w = (t0 + BLK - 1 - uw).astype(F32)
        s = lax.dot_general(qg_ref[g], kw_ref[pl.ds(wstart, WIN_KEYS), :], NT_DIMS, preferred_element_type=F32)
        for i in range(B_REP):
            rs = slice(i * BLK, (i + 1) * BLK)
            z = jnp.where(ok_w, s[rs] - slope[i] * dist_w, NEG)
            z_ref[rs, :] = z
            m_ref[rs, :] = _lanes(jnp.max(z, axis=-1, keepdims=True))

        @pl.when(b >= 0)
        def _():
            for i in range(B_REP):
                rs = slice(i * BLK, (i + 1) * BLK)
                p_ref[rs, :] = jnp.exp(z_ref[rs, :] - _tile_lanes(m_ref[rs, :], WIN_KEYS)).astype(BF16)
            vw = jnp.where(half, vw_ref[pl.ds(wstart, WIN_KEYS), :], one)
            ow = jnp.dot(p_ref[...], vw, preferred_element_type=F32)
            ow_ref[...] = ow / pltpu.roll(ow, B_HD, axis=1)

        m_ref[...] = jnp.full((rows, 2 * B_HD), NEG, F32)
        acc_ref[...] = jnp.zeros((rows, 2 * B_HD), F32)
        ntiles = (t0 + BLK + SEL_KT - 1) // SEL_KT

        def tile_body(it, carry):
            kt = ntiles - 1 - it

            @pl.when(jnp.bitwise_and(jnp.right_shift(bits, kt), 1) == 1)
            def _():
                k0 = pl.multiple_of(kt * SEL_KT, SEL_KT)
                us = k0 + lax.broadcasted_iota(jnp.int32, (1, SEL_KT), 1)
                selx = jnp.dot(sel_b, ex_ref[kt], preferred_element_type=F32)
                ok_s = (selx > 0.5) & (us <= tq)
                dist_s = (t0 + BLK - 1 - us).astype(F32)
                s = lax.dot_general(qg_ref[g], ks_ref[pl.ds(k0, SEL_KT), :], NT_DIMS, preferred_element_type=F32)
                for i in range(B_REP):
                    rs = slice(i * BLK, (i + 1) * BLK)
                    z = jnp.where(ok_s, s[rs] - slope[i] * dist_s, NEG)
                    z_ref[rs, :SEL_KT] = z
                    m_old = m_ref[rs, :]
                    m_new = jnp.maximum(m_old, _lanes(jnp.max(z, axis=-1, keepdims=True)))
                    a_ref[rs, :] = jnp.exp(m_old - m_new)
                    m_ref[rs, :] = m_new

                @pl.when(kt >= 0)
                def _():
                    for i in range(B_REP):
                        rs = slice(i * BLK, (i + 1) * BLK)
                        p_ref[rs, :SEL_KT] = jnp.exp(z_ref[rs, :SEL_KT]
                                                     - _tile_lanes(m_ref[rs, :], SEL_KT)).astype(BF16)
                    vs = jnp.where(half, vs_ref[pl.ds(k0, SEL_KT), :], one)
                    acc_ref[...] = a_ref[...] * acc_ref[...] + jnp.dot(p_ref[:, :SEL_KT], vs,
                                                                       preferred_element_type=F32)

            return carry

        lax.fori_loop(0, ntiles, tile_body, 0)

        for i in range(B_REP):
            rs = slice(i * BLK, (i + 1) * BLK)
            h = g * B_REP + i
            acc = acc_ref[rs, :]
            o_s = acc / pltpu.roll(acc, B_HD, axis=1)
            out = (gates[:, 3 * h:3 * h + 1] * oc_ref[g, rs, :] + gates[:, 3 * h + 1:3 * h + 2] * o_s
                   + gates[:, 3 * h + 2:3 * h + 3] * ow_ref[rs, :])
            cs = slice(i * 2 * B_HD, (i + 1) * 2 * B_HD)
            if g == 0:
                o_ref[:, cs] = out.astype(o_ref.dtype)
            else:
                o_ref[:, cs] = jnp.where(lo, o_ref[:, cs], out.astype(o_ref.dtype))


def _nsa_attention(proj, kc, vc, slopes):
    t = proj.shape[0]
    assert t % SEL_KT == 0 and SLC_LEN == 1 << SLC_SHIFT and SEL_KT == 1 << SEL_KT_SHIFT
    ncmp = kc.shape[0]
    nslc = t // SLC_LEN
    ntile = t // SEL_KT
    assert ntile <= 24
    c_start = jnp.arange(ncmp)[None, :] * CMP_STRIDE
    s_start = jnp.arange(nslc)[:, None] * SLC_LEN
    overlap = jnp.clip(jnp.minimum(c_start + CMP_LEN, s_start + SLC_LEN) - jnp.maximum(c_start, s_start),
                       0, None).astype(F32) / CMP_LEN
    overlap = jnp.where(jnp.arange(ncmp)[None, :] < ncmp - 1, overlap, 0.0).astype(BF16)
    key_blk = (jnp.arange(ntile)[:, None, None] * SEL_KT + jnp.arange(SEL_KT)[None, None, :]) // SLC_LEN
    expand = (key_blk == jnp.arange(nslc)[None, :, None]).astype(BF16)
    w2 = 2 * B_HD
    qcol = COL_BQ // (4 * w2)
    kvcol = COL_BKV // w2
    full = lambda c: pl.BlockSpec((t, w2), lambda b: (0, c))
    rows = B_REP * BLK
    return pl.pallas_call(
        _nsa_kernel,
        grid=(t // BLK,),
        in_specs=[pl.BlockSpec(memory_space=pltpu.SMEM),
                  pl.BlockSpec((BLK, 4 * w2), lambda b: (b, qcol)),
                  pl.BlockSpec((BLK, 4 * w2), lambda b: (b, qcol + 1)),
                  pl.BlockSpec((BLK, w2), lambda b: (b, COL_BG // w2)),
                  pl.BlockSpec((ncmp, w2), lambda b: (0, 0)),
                  pl.BlockSpec((ncmp, w2), lambda b: (0, 0)),
                  full(kvcol + 2), full(kvcol + 3), full(kvcol + 4), full(kvcol + 5),
                  pl.BlockSpec((nslc, ncmp), lambda b: (0, 0)),
                  pl.BlockSpec((ntile, nslc, SEL_KT), lambda b: (0, 0, 0))],
        out_specs=pl.BlockSpec((BLK, B_HEADS * B_HD), lambda b: (b, 0)),
        out_shape=jax.ShapeDtypeStruct((t, B_HEADS * B_HD), BF16),
        scratch_shapes=[pltpu.VMEM((B_KV, rows, w2), BF16),
                        pltpu.VMEM((B_KV, rows, w2), F32),
                        pltpu.VMEM((rows, w2), F32),
                        pltpu.VMEM((rows, w2), F32),
                        pltpu.VMEM((rows, w2), F32),
                        pltpu.VMEM((rows, w2), F32),
                        pltpu.VMEM((rows, WIN_KEYS), F32),
                        pltpu.VMEM((rows, WIN_KEYS), BF16),
                        pltpu.VMEM((B_KV, BLK, ncmp), F32)],
        compiler_params=_cparams(("parallel",)),
        name="nsa_attention",
    )(slopes, proj, proj, proj, kc, vc, proj, proj, proj, proj, overlap, expand)


MERGE_KC = 512


def _merge_kernel(o0_ref, o1_ref, o2_ref, l0_ref, l1_ref, l2_ref, yb_ref, x_ref, gn_ref, wg0_ref, wg1_ref,
                  wa_ref, wb_ref, wo_ref, h_ref, ya_ref, xn_ref):
    kc = pl.program_id(1)

    @pl.when(kc == 0)
    def _():
        for hd in range(A_HEADS):
            l0, l1, l2 = l0_ref[hd], l1_ref[hd], l2_ref[hd]
            m = jnp.maximum(jnp.maximum(l0, l1), l2)
            e0, e1, e2 = jnp.exp(l0 - m), jnp.exp(l1 - m), jnp.exp(l2 - m)
            den = e0 + e1 + e2
            ya = (e0 / den) * o0_ref[hd] + (e1 / den) * o1_ref[hd] + (e2 / den) * o2_ref[hd]
            ya_ref[:, hd * A_HD:(hd + 1) * A_HD] = ya.astype(BF16)
        x = x_ref[...]
        h_ref[...] = x
        xn_ref[...] = _rms(x, gn_ref[...]).astype(BF16)

    g0 = _sigmoid(jnp.dot(xn_ref[...], wg0_ref[...], preferred_element_type=F32))
    g1 = _sigmoid(jnp.dot(xn_ref[...], wg1_ref[...], preferred_element_type=F32))
    ua = jnp.dot(ya_ref[...], wa_ref[...], preferred_element_type=F32)
    ub = jnp.dot(yb_ref[...], wb_ref[...], preferred_element_type=F32)
    mix = g0 * ua + g1 * ub
    h_ref[...] += jnp.dot(mix.astype(BF16), wo_ref[...], preferred_element_type=F32)


def _merge(x, norm_mix, w_gate, dil_o, dil_lse, yb, wa, wb, wo, tm):
    t, d = x.shape
    nk = d // MERGE_KC
    row = lambda w: pl.BlockSpec((tm, w), lambda i, k: (i, 0))
    return pl.pallas_call(
        _merge_kernel,
        grid=(t // tm, nk),
        in_specs=[pl.BlockSpec((A_HEADS, tm, A_HD), lambda i, k: (0, i, 0))] * 6 + [row(B_HEADS * B_HD), row(d),
                  pl.BlockSpec((1, d), lambda i, k: (0, 0)),
                  pl.BlockSpec((d, MERGE_KC), lambda i, k: (0, k)),
                  pl.BlockSpec((d, MERGE_KC), lambda i, k: (0, nk + k)),
                  pl.BlockSpec((A_OUT, MERGE_KC), lambda i, k: (0, k)),
                  pl.BlockSpec((B_HEADS * B_HD, MERGE_KC), lambda i, k: (0, k)),
                  pl.BlockSpec((MERGE_KC, d), lambda i, k: (k, 0))],
        out_specs=pl.BlockSpec((tm, d), lambda i, k: (i, 0)),
        out_shape=jax.ShapeDtypeStruct((t, d), F32),
        scratch_shapes=[pltpu.VMEM((tm, A_OUT), BF16), pltpu.VMEM((tm, d), BF16)],
        compiler_params=_cparams(("parallel", "arbitrary")),
        name="merge",
    )(*dil_o, *dil_lse, yb, x, norm_mix.reshape(1, d), w_gate, w_gate, wa, wb, wo)


def _cross_kernel(h_ref, gx_ref, gf_ref, wq_ref, kv_ref, wo_ref, h2_ref, xn_ref):
    h = h_ref[...]
    hn = _rms(h, gx_ref[...]).astype(BF16)
    q = jnp.dot(hn, wq_ref[...], preferred_element_type=F32).astype(BF16)
    kv = kv_ref[...]
    nh = X_HEADS * X_HD
    outs = []
    for hd in range(X_HEADS):
        sl = slice(hd * X_HD, (hd + 1) * X_HD)
        s = lax.dot_general(q[:, sl], kv[:, sl], NT_DIMS, preferred_element_type=F32) * (X_HD ** -0.5)
        m = jnp.max(s, axis=-1, keepdims=True)
        p = jnp.exp(s - m)
        p = p / jnp.sum(p, axis=-1, keepdims=True)
        outs.append(jnp.dot(p.astype(BF16), kv[:, nh + hd * X_HD:nh + (hd + 1) * X_HD],
                            preferred_element_type=F32))
    o = jnp.concatenate(outs, axis=1).astype(BF16)
    h2 = h + jnp.dot(o, wo_ref[...], preferred_element_type=F32)
    h2_ref[...] = h2
    xn_ref[...] = _rms(h2, gf_ref[...]).astype(BF16)


def _cross(h1, norm_x, norm_ffn, wq, kv, wo, tm):
    t, d = h1.shape
    nm = kv.shape[0]
    nh = X_HEADS * X_HD
    const = lambda shape: pl.BlockSpec(shape, lambda i: (0, 0))
    return pl.pallas_call(
        _cross_kernel,
        grid=(t // tm,),
        in_specs=[pl.BlockSpec((tm, d), lambda i: (i, 0)), const((1, d)), const((1, d)),
                  const((d, nh)), const((nm, 2 * nh)), const((nh, d))],
        out_specs=[pl.BlockSpec((tm, d), lambda i: (i, 0)), pl.BlockSpec((tm, d), lambda i: (i, 0))],
        out_shape=[jax.ShapeDtypeStruct((t, d), F32), jax.ShapeDtypeStruct((t, d), BF16)],
        compiler_params=_cparams(("parallel",)),
        name="cross_attn",
    )(h1, norm_x.reshape(1, d), norm_ffn.reshape(1, d), wq, kv, wo)


ROUTE_TB = 128
ROUTE_HPI = 4
MAP_TOK = 16


def _top16_rows(s, rowf, nrows):
    vals, idxs = [], []
    for _ in range(PEER_TOPK):
        mx = jnp.max(s, axis=0, keepdims=True)
        idx = jnp.min(jnp.where(s == mx, rowf, float(nrows)), axis=0, keepdims=True)
        vals.append(mx)
        idxs.append(idx)
        s = jnp.where(rowf == idx, REMOVED, s)
    return vals, idxs


def _route_kernel(xn_ref, wq_ref, k1_ref, k2_ref, w_ref, q_scr, g_scr, e1_scr, e2_scr, gt_scr, e1t_scr, e2t_scr,
                  wt_scr):
    tb = xn_ref.shape[0]
    half = PEER_DK // 2
    q = jnp.dot(xn_ref[...], wq_ref[...], preferred_element_type=F32).astype(BF16)
    for c in range(2 * PEER_HEADS):
        q_scr[c] = q[:, c * half:(c + 1) * half]
    nl = ROUTE_HPI * tb
    rowf = lax.broadcasted_iota(jnp.int32, (N_KEYS, nl), 0).astype(F32)
    nc = 8 * 10
    rowc = lax.broadcasted_iota(jnp.int32, (nc, nl), 0).astype(F32)

    def head_body(hp, carry):
        h0 = hp * ROUTE_HPI
        s1 = jnp.concatenate([lax.dot_general(k1_ref[...], q_scr[2 * (h0 + u)], NT_DIMS, preferred_element_type=F32)
                              for u in range(ROUTE_HPI)], axis=1)
        s2 = jnp.concatenate([lax.dot_general(k2_ref[...], q_scr[2 * (h0 + u) + 1], NT_DIMS,
                                              preferred_element_type=F32) for u in range(ROUTE_HPI)], axis=1)
        v1, i1 = _top16_rows(s1, rowf, N_KEYS)
        v2, i2 = _top16_rows(s2, rowf, N_KEYS)
        sub8 = lax.broadcasted_iota(jnp.int32, (8, nl), 0)
        v2lo, v2hi = jnp.concatenate(v2[:8], axis=0), jnp.concatenate(v2[8:], axis=0)
        i2lo, i2hi = jnp.concatenate(i2[:8], axis=0), jnp.concatenate(i2[8:], axis=0)
        cand_p = [v1[0] + v2lo, v1[0] + v2hi]
        e1_p = [jnp.broadcast_to(i1[0], (8, nl))] * 2
        e2_p = [i2lo, i2hi]
        for a in range(1, 8):
            nb = PEER_TOPK // (a + 1)
            piece = v1[a] + v2lo
            cand_p.append(piece if nb >= 8 else jnp.where(sub8 < nb, piece, REMOVED))
            e1_p.append(jnp.broadcast_to(i1[a], (8, nl)))
            e2_p.append(i2lo)
        cand_p.append(jnp.concatenate(v1[8:], axis=0) + v2[0])
        e1_p.append(jnp.concatenate(i1[8:], axis=0))
        e2_p.append(jnp.broadcast_to(i2[0], (8, nl)))
        cand = jnp.concatenate(cand_p, axis=0)
        e1c = jnp.concatenate(e1_p, axis=0)
        e2c = jnp.concatenate(e2_p, axis=0)
        vs, e1, e2 = [], [], []
        for _ in range(PEER_TOPK):
            mx = jnp.max(cand, axis=0, keepdims=True)
            idx = jnp.min(jnp.where(cand == mx, rowc, float(nc)), axis=0, keepdims=True)
            pick = rowc == idx
            vs.append(mx)
            e1.append(jnp.sum(jnp.where(pick, e1c, 0.0), axis=0, keepdims=True))
            e2.append(jnp.sum(jnp.where(pick, e2c, 0.0), axis=0, keepdims=True))
            cand = jnp.where(pick, REMOVED, cand)
        vsa = jnp.concatenate(vs, axis=0)
        ex = jnp.exp(vsa - vs[0])
        gate = ex / jnp.sum(ex, axis=0, keepdims=True)
        e1a, e2a = jnp.concatenate(e1, axis=0), jnp.concatenate(e2, axis=0)
        for u in range(ROUTE_HPI):
            r0 = pl.multiple_of((h0 + u) * PEER_TOPK, PEER_TOPK)
            ls = slice(u * tb, (u + 1) * tb)
            g_scr[pl.ds(r0, PEER_TOPK), :] = gate[:, ls]
            e1_scr[pl.ds(r0, PEER_TOPK), :] = e1a[:, ls]
            e2_scr[pl.ds(r0, PEER_TOPK), :] = e2a[:, ls]
        return carry

    lax.fori_loop(0, PEER_HEADS // ROUTE_HPI, head_body, 0)

    gt_scr[...] = g_scr[...].T
    e1t_scr[...] = e1_scr[...].T
    e2t_scr[...] = e2_scr[...].T
    sub = lax.broadcasted_iota(jnp.int32, (N_KEYS, PEER_HEADS * PEER_TOPK), 0).astype(F32)

    def group_body(grp, carry):
        shape = (N_KEYS, PEER_HEADS * PEER_TOPK)
        for tt in range(MAP_TOK):
            t = grp * MAP_TOK + tt
            g_row = jnp.broadcast_to(gt_scr[pl.ds(t, 1), :], shape)
            e1_row = jnp.broadcast_to(e1t_scr[pl.ds(t, 1), :], shape)
            e2_row = jnp.broadcast_to(e2t_scr[pl.ds(t, 1), :], shape)
            a = jnp.where(e1_row == sub, g_row, 0.0).astype(BF16)
            bm = jnp.where(e2_row == sub, 1.0, 0.0).astype(BF16)
            wt_scr[pl.ds(tt, N_KEYS, stride=MAP_TOK), :] = lax.dot_general(a, bm, NT_DIMS,
                                                                          preferred_element_type=F32)
        w_ref[grp] = wt_scr[...].astype(w_ref.dtype)
        return carry

    lax.fori_loop(0, tb // MAP_TOK, group_body, 0)


def _peer_route(xn, wq, k1, k2):
    t, d = xn.shape
    tb = ROUTE_TB
    half = PEER_DK // 2
    slots = PEER_HEADS * PEER_TOPK
    const = lambda shape: pl.BlockSpec(shape, lambda i: (0,) * len(shape))
    return pl.pallas_call(
        _route_kernel,
        grid=(t // tb,),
        in_specs=[pl.BlockSpec((tb, d), lambda i: (i, 0)), const((d, PEER_HEADS * PEER_DK)),
                  const((N_KEYS, half)), const((N_KEYS, half))],
        out_specs=pl.BlockSpec((tb // MAP_TOK, N_KEYS * MAP_TOK, N_KEYS), lambda i: (i, 0, 0)),
        out_shape=jax.ShapeDtypeStruct((t // MAP_TOK, N_KEYS * MAP_TOK, N_KEYS), BF16),
        scratch_shapes=[pltpu.VMEM((2 * PEER_HEADS, tb, half), BF16),
                        pltpu.VMEM((slots, tb), F32), pltpu.VMEM((slots, tb), F32), pltpu.VMEM((slots, tb), F32),
                        pltpu.VMEM((tb, slots), F32), pltpu.VMEM((tb, slots), F32), pltpu.VMEM((tb, slots), F32),
                        pltpu.VMEM((N_KEYS * MAP_TOK, N_KEYS), F32)],
        compiler_params=_cparams(("parallel",)),
        name="peer_route",
    )(xn, wq, k1, k2)


def _castT_kernel(u_ref, o_ref):
    o_ref[...] = u_ref[...].T.astype(o_ref.dtype)


def _cast_transpose(u, te, td):
    ne, d = u.shape
    return pl.pallas_call(
        _castT_kernel,
        grid=(ne // te, d // td),
        in_specs=[pl.BlockSpec((te, td), lambda i, j: (i, j))],
        out_specs=pl.BlockSpec((td, te), lambda i, j: (j, i)),
        out_shape=jax.ShapeDtypeStruct((d, ne), BF16),
        compiler_params=_cparams(("parallel", "parallel")),
        name="expert_u_transpose",
    )(u)


def _dense_kernel(xn_ref, ut_ref, w_ref, v_ref, h_ref, gf_ref, o_ref, acc_ref, aw_ref):
    e = pl.program_id(1)

    @pl.when(e == 0)
    def _():
        acc_ref[...] = jnp.zeros_like(acc_ref)

    tb, eb = aw_ref.shape
    a = _gelu(jnp.dot(xn_ref[...], ut_ref[...], preferred_element_type=F32))
    for bb in range(tb // MAP_TOK):
        rs = slice(bb * MAP_TOK, (bb + 1) * MAP_TOK)
        for j in range(eb // N_KEYS):
            cs = slice(j * N_KEYS, (j + 1) * N_KEYS)
            aw_ref[rs, cs] = (a[rs, cs] * w_ref[bb, j * MAP_TOK:(j + 1) * MAP_TOK, :].astype(F32)).astype(BF16)
    acc_ref[...] += jnp.dot(aw_ref[...], v_ref[...], preferred_element_type=F32)

    @pl.when(e == pl.num_programs(1) - 1)
    def _():
        o_ref[...] = _rms(h_ref[...] + acc_ref[...], gf_ref[...]).astype(o_ref.dtype)


def _peer_dense(xn, ut, wmap, v, h2, norm_final, tb, eb):
    t, d = xn.shape
    ne = ut.shape[1]
    return pl.pallas_call(
        _dense_kernel,
        grid=(t // tb, ne // eb),
        in_specs=[pl.BlockSpec((tb, d), lambda i, e: (i, 0)),
                  pl.BlockSpec((d, eb), lambda i, e: (0, e)),
                  pl.BlockSpec((tb // MAP_TOK, eb // N_KEYS * MAP_TOK, N_KEYS), lambda i, e: (i, e, 0)),
                  pl.BlockSpec((eb, d), lambda i, e: (e, 0)),
                  pl.BlockSpec((tb, d), lambda i, e: (i, 0)),
                  pl.BlockSpec((1, d), lambda i, e: (0, 0))],
        out_specs=pl.BlockSpec((tb, d), lambda i, e: (i, 0)),
        out_shape=jax.ShapeDtypeStruct((t, d), F32),
        scratch_shapes=[pltpu.VMEM((tb, d), F32), pltpu.VMEM((tb, eb), BF16)],
        compiler_params=_cparams(("parallel", "arbitrary")),
        name="peer_dense",
    )(xn, ut, wmap, v, h2, norm_final.reshape(1, d))


def _alibi_slopes(n):
    return jnp.power(2.0, -8.0 * (jnp.arange(n, dtype=F32) + 1.0) / n)


def _block(x, mem, norm_mix, w_in, w_cmp_k1, w_cmp_k2, pe_cmp_k, w_cmp_v1, w_cmp_v2, pe_cmp_v,
           w_up_a, w_up_b, w_out, norm_x, norm_mem, w_xq, w_xkv, w_xo, norm_ffn, w_pq,
           sub_keys1, sub_keys2, expert_u, expert_v, norm_out, out_dtype):
    t, d = x.shape
    assert d == D_MODEL and w_in.shape[1] == COL_BG + B_GATE_COLS + 2 * D_MODEL
    tm = min(1024, t)

    proj, *strided = _in_projection(x, norm_mix, w_in, tm)
    w_gate = w_in[:, COL_BG + B_GATE_COLS:].astype(BF16)

    slopes_a = _alibi_slopes(A_GROUPS * A_HEADS)
    dil = [_dilated_group(src, slopes_a, g, t) for g, src in enumerate([proj] + strided)]

    kc, vc = _nsa_compress(proj, pe_cmp_k, pe_cmp_v, w_cmp_k1, w_cmp_k2, w_cmp_v1, w_cmp_v2)
    yb = _nsa_attention(proj, kc, vc, _alibi_slopes(B_HEADS))

    wb = w_up_b.reshape(B_KV, B_REP, B_HD, d).transpose(1, 0, 2, 3).reshape(B_HEADS * B_HD, d)
    h1 = _merge(x, norm_mix, w_gate, [o for o, _ in dil], [l for _, l in dil], yb,
                w_up_a.astype(BF16), wb.astype(BF16), w_out.astype(BF16), min(512, t))

    kvm = _norm_matmul(mem, norm_mem, w_xkv.astype(BF16), mem.shape[0], 512, BF16)
    h2, xn = _cross(h1, norm_x, norm_ffn, w_xq.astype(BF16), kvm, w_xo.astype(BF16), min(512, t))

    wmap = _peer_route(xn, w_pq.astype(BF16), sub_keys1.astype(BF16), sub_keys2.astype(BF16))
    return _peer_dense(xn, _cast_transpose(expert_u, 512, 1024), wmap, expert_v.astype(BF16), h2, norm_out,
                       min(512, t), 1024).astype(out_dtype)


@jax.jit
def kernel(x, mem, norm_mix, w_in, w_cmp_k1, w_cmp_k2, pe_cmp_k, w_cmp_v1, w_cmp_v2, pe_cmp_v, w_up_a, w_up_b,
           w_out, norm_x, norm_mem, w_xq, w_xkv, w_xo, norm_ffn, w_pq, sub_keys1, sub_keys2, expert_u,
           expert_v, norm_final):
    assert x.shape[0] == 1 and mem.shape[0] == 1 and norm_mix.shape[0] == 1
    out = _block(x[0], mem[0], norm_mix[0], w_in[0], w_cmp_k1[0], w_cmp_k2[0], pe_cmp_k[0], w_cmp_v1[0],
                 w_cmp_v2[0], pe_cmp_v[0], w_up_a[0], w_up_b[0], w_out[0], norm_x[0], norm_mem[0], w_xq[0],
                 w_xkv[0], w_xo[0], norm_ffn[0], w_pq[0], sub_keys1[0], sub_keys2[0], expert_u[0],
                 expert_v[0], norm_final, x.dtype)
    return out[None]
```

```python
import functools

import jax
import jax.numpy as jnp
from jax import lax
from jax.experimental import pallas as pl
from jax.experimental.pallas import tpu as pltpu

F32 = jnp.float32
BF16 = jnp.bfloat16

D_MODEL = 2048
BLK = 128
EPS = 1e-6
NEG = -1e30
TINY = 1e-30
FORCE = 1e4
REMOVED = -3e38

A_DILATIONS = (1, 4, 16)
A_WINDOWS = (128, 512, 2048)
A_GROUPS = 3
A_HEADS = 4
A_HD = 128
B_HEADS = 16
B_KV = 2
B_REP = B_HEADS // B_KV
B_HD = 64
CMP_LEN = 32
CMP_STRIDE = 16
CMP_HIDDEN = 128
SLC_LEN = 64
SLC_SHIFT = 6
N_SEL = 16
WIN = 512
X_HEADS = 4
X_HD = 128
PEER_HEADS = 8
PEER_DK = 256
N_KEYS = 128
N_EXPERTS = N_KEYS * N_KEYS
PEER_TOPK = 16

A_QKV_COLS = 3 * A_GROUPS * A_HEADS * A_HD
A_OUT = A_HEADS * A_HD
B_Q_COLS = B_HEADS * B_HD
B_KV_COLS = 3 * 2 * B_KV * B_HD
B_GATE_COLS = B_HEADS * 3

COL_A = 0
COL_BQ = COL_A + A_QKV_COLS
COL_BKV = COL_BQ + B_Q_COLS
COL_BG = COL_BKV + B_KV_COLS
PROJ_COLS = 6656

VMEM_LIMIT = 56 * 1024 * 1024

NT_DIMS = (((1,), (1,)), ((), ()))


def _cparams(sem):
    return pltpu.CompilerParams(dimension_semantics=sem, vmem_limit_bytes=VMEM_LIMIT)


def _rms(x, g):
    ms = jnp.mean(x * x, axis=-1, keepdims=True)
    return x * lax.rsqrt(ms + EPS) * g


_GELU_K1 = -2.0 * 0.7978845608028654 * 1.4426950408889634
_GELU_K2 = _GELU_K1 * 0.044715


def _gelu(x):
    return x / (1.0 + jnp.exp2(x * (_GELU_K1 + _GELU_K2 * (x * x))))


def _sigmoid(x):
    return 1.0 / (1.0 + jnp.exp(-x))


def _nmm_kernel(x_ref, g_ref, w_ref, o_ref, xn_ref):
    @pl.when(pl.program_id(1) == 0)
    def _():
        xn_ref[...] = _rms(x_ref[...].astype(F32), g_ref[...]).astype(BF16)

    o_ref[...] = jnp.dot(xn_ref[...], w_ref[...], preferred_element_type=F32).astype(o_ref.dtype)


def _norm_matmul(x, gain, w, tm, tn, out_dtype):
    t, d = x.shape
    n = w.shape[1]
    return pl.pallas_call(
        _nmm_kernel,
        grid=(t // tm, n // tn),
        in_specs=[pl.BlockSpec((tm, d), lambda i, j: (i, 0)),
                  pl.BlockSpec((1, d), lambda i, j: (0, 0)),
                  pl.BlockSpec((d, tn), lambda i, j: (0, j))],
        out_specs=pl.BlockSpec((tm, tn), lambda i, j: (i, j)),
        out_shape=jax.ShapeDtypeStruct((t, n), out_dtype),
        scratch_shapes=[pltpu.VMEM((tm, d), BF16)],
        compiler_params=_cparams(("parallel", "arbitrary")),
        name="norm_matmul",
    )(x, gain.reshape(1, d), w)


IN_TN = A_OUT


def _inproj_kernel(x_ref, g_ref, w_ref, proj_ref, *rest):
    dil_refs, (xn_ref, res_ref) = rest[:-2], rest[-2:]
    j = pl.program_id(1)
    tm = x_ref.shape[0]

    @pl.when(j == 0)
    def _():
        xn_ref[...] = _rms(x_ref[...], g_ref[...]).astype(BF16)

    res = jnp.dot(xn_ref[...], w_ref[...].astype(BF16), preferred_element_type=F32)
    proj_ref[...] = res.astype(proj_ref.dtype)
    for out_ref, group in zip(dil_refs, range(1, A_GROUPS)):
        dil = A_DILATIONS[group]
        for s in range(3):
            @pl.when(j == s * A_GROUPS + group)
            def _(out_ref=out_ref, dil=dil, s=s):
                for c in range(IN_TN // A_HD):
                    res_ref[c] = res[:, c * A_HD:(c + 1) * A_HD]
                for r in range(dil):
                    for c in range(IN_TN // A_HD):
                        c0 = (r * 3 + s) * IN_TN + c * A_HD
                        out_ref[:, c0:c0 + A_HD] = res_ref[c, pl.ds(r, tm // dil, stride=dil), :].astype(out_ref.dtype)


def _in_projection(x, gain, w_in, tm):
    t, d = x.shape
    groups = range(1, A_GROUPS)
    assert all(tm % A_DILATIONS[g] == 0 and (tm // A_DILATIONS[g]) % 16 == 0 for g in groups)
    dil_specs = [pl.BlockSpec((tm // A_DILATIONS[g], A_DILATIONS[g] * 3 * A_OUT), lambda i, j: (i, 0)) for g in groups]
    dil_shapes = [jax.ShapeDtypeStruct((t // A_DILATIONS[g], A_DILATIONS[g] * 3 * A_OUT), BF16) for g in groups]
    return pl.pallas_call(
        _inproj_kernel,
        grid=(t // tm, PROJ_COLS // IN_TN),
        in_specs=[pl.BlockSpec((tm, d), lambda i, j: (i, 0)),
                  pl.BlockSpec((1, d), lambda i, j: (0, 0)),
                  pl.BlockSpec((d, IN_TN), lambda i, j: (0, j))],
        out_specs=[pl.BlockSpec((tm, IN_TN), lambda i, j: (i, j))] + dil_specs,
        out_shape=[jax.ShapeDtypeStruct((t, PROJ_COLS), BF16)] + dil_shapes,
        scratch_shapes=[pltpu.VMEM((tm, d), BF16), pltpu.VMEM((IN_TN // A_HD, tm, A_HD), F32)],
        compiler_params=_cparams(("parallel", "arbitrary")),
        name="in_projection",
    )(x, gain.reshape(1, d), w_in)


def _gate_w_kernel(a_ref, b_ref, o_ref, *, shift):
    both = jnp.concatenate([a_ref[...], b_ref[...]], axis=1)
    o_ref[...] = both[:, shift:shift + o_ref.shape[1]].astype(o_ref.dtype)


def _gate_weights(w_in, first, tn):
    d, n_in = w_in.shape
    n = n_in - first
    blk0, shift = divmod(first, tn)
    assert n % tn == 0 and 0 < shift and (blk0 + n // tn) * tn + shift == n_in
    return pl.pallas_call(
        functools.partial(_gate_w_kernel, shift=shift),
        grid=(n // tn,),
        in_specs=[pl.BlockSpec((d, tn), lambda j: (0, blk0 + j)),
                  pl.BlockSpec((d, tn), lambda j: (0, blk0 + j + 1))],
        out_specs=pl.BlockSpec((d, tn), lambda j: (0, j)),
        out_shape=jax.ShapeDtypeStruct((d, n), BF16),
        compiler_params=_cparams(("parallel",)),
        name="gate_weights",
    )(w_in, w_in)


def _dil_kernel(slopes_ref, q_ref, kp_ref, ko_ref, vp_ref, vo_ref, o_ref, lse_ref, z_ref, m_ref, *, dil, qb, group):
    b = pl.program_id(0)
    rows = pl.ds(pl.program_id(1), qb, stride=dil) if dil > 1 else slice(None)
    nk = qb + BLK
    q = q_ref[...]
    k = jnp.concatenate([kp_ref[...], ko_ref[...]], axis=0)
    i = lax.broadcasted_iota(jnp.int32, (qb, nk), 0)
    j = lax.broadcasted_iota(jnp.int32, (qb, nk), 1)
    rel = i + BLK - j
    jmin = jnp.where(b > 0, 0, BLK)
    ok = (rel >= 0) & (rel <= BLK) & (j >= jmin)
    dist = (rel * dil).astype(F32)
    scale = A_HD ** -0.5
    for h in range(A_HEADS):
        sl = slice(h * A_HD, (h + 1) * A_HD)
        s = lax.dot_general(q[:, sl], k[:, sl], NT_DIMS, preferred_element_type=F32) * scale
        s = jnp.where(ok, s - slopes_ref[group * A_HEADS + h] * dist, NEG)
        z_ref[h] = s
        m_ref[h] = jnp.broadcast_to(jnp.max(s, axis=-1, keepdims=True), (qb, A_HD))

    @pl.when(b >= 0)
    def _():
        v = jnp.concatenate([vp_ref[...], vo_ref[...]], axis=0)
        for h in range(A_HEADS):
            sl = slice(h * A_HD, (h + 1) * A_HD)
            m = m_ref[h]
            p = jnp.exp(z_ref[h] - jnp.concatenate([m] * (nk // A_HD), axis=1))
            den = jnp.sum(p, axis=-1, keepdims=True)
            o_ref[h, rows, :] = jnp.dot(p.astype(BF16), v[:, sl], preferred_element_type=F32) / den
            lse_ref[h, rows, :] = m + jnp.log(den)


def _dilated_group(src, slopes, group, t):
    dil = A_DILATIONS[group]
    assert A_WINDOWS[group] // dil == BLK
    l = t // dil
    qb = min(256 if dil < 16 else BLK, l)
    assert l % qb == 0 and qb % BLK == 0
    stride, first, step = (PROJ_COLS // A_OUT, group, A_GROUPS) if dil == 1 else (3, 0, 1)

    def col(s):
        return lambda b, r: (b, r * stride + first + s * step)

    def colp(s):
        return lambda b, r: (jnp.maximum(b * (qb // BLK) - 1, 0), r * stride + first + s * step)

    out_spec = pl.BlockSpec((A_HEADS, qb * dil, A_HD), lambda b, r: (0, b, 0))
    out_shape = jax.ShapeDtypeStruct((A_HEADS, t, A_HD), F32)
    return pl.pallas_call(
        functools.partial(_dil_kernel, dil=dil, qb=qb, group=group),
        grid=(l // qb, dil),
        in_specs=[pl.BlockSpec(memory_space=pltpu.SMEM),
                  pl.BlockSpec((qb, A_OUT), col(0)),
                  pl.BlockSpec((BLK, A_OUT), colp(1)),
                  pl.BlockSpec((qb, A_OUT), col(1)),
                  pl.BlockSpec((BLK, A_OUT), colp(2)),
                  pl.BlockSpec((qb, A_OUT), col(2))],
        out_specs=[out_spec, out_spec],
        out_shape=[out_shape, out_shape],
        scratch_shapes=[pltpu.VMEM((A_HEADS, qb, qb + BLK), F32), pltpu.VMEM((A_HEADS, qb, A_HD), F32)],
        compiler_params=_cparams(("parallel", "arbitrary")),
        name=f"dilated_attn_g{group}",
    )(slopes, src, src, src, src, src)


def _cmp_kernel(a_ref, pe_ref, w1_ref, w2_ref, o_ref):
    a = a_ref[0]
    n16 = a.shape[0]
    nxt = pltpu.roll(a, n16 - 1, axis=0)
    blocks = jnp.concatenate([a, nxt], axis=1) + pe_ref[0]
    hid = _gelu(jnp.dot(blocks.astype(BF16), w1_ref[0], preferred_element_type=F32))
    out = jnp.dot(hid.astype(BF16), w2_ref[0], preferred_element_type=F32)
    row = lax.broadcasted_iota(jnp.int32, out.shape, 0)
    o_ref[0] = jnp.where(row < n16 - 1, out, 0.0)


def _nsa_compress(proj, pe_k, pe_v, w_k1, w_k2, w_v1, w_v2):
    t = proj.shape[0]
    assert CMP_LEN == 2 * CMP_STRIDE and t % CMP_STRIDE == 0
    n16 = t // CMP_STRIDE
    kv = proj[:, COL_BKV:COL_BKV + 2 * B_KV * B_HD].astype(F32)
    a4 = kv.reshape(t, 2 * B_KV, B_HD).transpose(1, 0, 2).reshape(2 * B_KV, n16, CMP_STRIDE * B_HD)
    pe = jnp.stack([pe_k.reshape(1, -1), pe_v.reshape(1, -1)])
    w1 = jnp.stack([w_k1, w_v1]).astype(BF16)
    w2 = jnp.stack([w_k2, w_v2]).astype(BF16)
    kd = CMP_LEN * B_HD
    out = pl.pallas_call(
        _cmp_kernel,
        grid=(2 * B_KV,),
        in_specs=[pl.BlockSpec((1, n16, CMP_STRIDE * B_HD), lambda i: (i, 0, 0)),
                  pl.BlockSpec((1, 1, kd), lambda i: (i // B_KV, 0, 0)),
                  pl.BlockSpec((1, kd, CMP_HIDDEN), lambda i: (i // B_KV, 0, 0)),
                  pl.BlockSpec((1, CMP_HIDDEN, B_HD), lambda i: (i // B_KV, 0, 0))],
        out_specs=pl.BlockSpec((1, n16, B_HD), lambda i: (i, 0, 0)),
        out_shape=jax.ShapeDtypeStruct((2 * B_KV, n16, B_HD), F32),
        compiler_params=_cparams(("parallel",)),
        name="nsa_compress",
    )(a4, pe, w1, w2)
    kc = jnp.concatenate([out[0], out[1]], axis=-1).astype(BF16)
    vc = jnp.concatenate([out[2], out[3]], axis=-1).astype(BF16)
    return kc, vc


SEL_KT = 512
SEL_KT_SHIFT = 9
WIN_KEYS = WIN + BLK


def _lanes(col):
    return jnp.broadcast_to(col, (col.shape[0], 2 * B_HD))


def _tile_lanes(rep, n):
    return jnp.concatenate([rep] * (n // (2 * B_HD)), axis=1)


def _nsa_kernel(slopes_ref, q0_ref, q1_ref, gate_ref, kc_ref, vc_ref, ks_ref, vs_ref, kw_ref, vw_ref,
                ovt_ref, ex_ref, o_ref, qg_ref, oc_ref, ow_ref, acc_ref, m_ref, a_ref, z_ref, p_ref, ps_ref):
    b = pl.program_id(0)
    t0 = b * BLK
    rows = B_REP * BLK
    ncmp = kc_ref.shape[0]
    nslc = ovt_ref.shape[0]
    ntile_all = ex_ref.shape[0]
    lane = lax.broadcasted_iota(jnp.int32, (1, 2 * B_HD), 1)
    lo = lane < B_HD
    tq = t0 + lax.broadcasted_iota(jnp.int32, (BLK, 1), 0)
    tq_lane = t0 + lax.broadcasted_iota(jnp.int32, (1, BLK), 1)
    qscale = jnp.asarray(B_HD ** -0.5, BF16)
    qnat = [q0_ref[...] * qscale, q1_ref[...] * qscale]
    gates = _sigmoid(gate_ref[...].astype(F32))
    zero = jnp.zeros((), BF16)
    one = jnp.ones((), BF16)

    cend = lax.broadcasted_iota(jnp.int32, (1, ncmp), 1) * CMP_STRIDE + (CMP_LEN - 1)
    ok_c = cend <= tq
    dist_c = (t0 + BLK - 1 - cend).astype(F32)
    halves = [lo, jnp.logical_not(lo)]
    slopes = [[slopes_ref[g * B_REP + i] for i in range(B_REP)] for g in range(B_KV)]

    for g in range(B_KV):
        slope = slopes[g]
        for i in range(B_REP):
            x = qnat[g][:, (i // 2) * 2 * B_HD:(i // 2 + 1) * 2 * B_HD]
            if i % 2 != g:
                x = jnp.concatenate([x[:, B_HD:], x[:, :B_HD]], axis=1)
            qg_ref[g, i * BLK:(i + 1) * BLK, :] = jnp.where(halves[g], x, zero)
        s = lax.dot_general(qg_ref[g], kc_ref[...], NT_DIMS, preferred_element_type=F32)
        for i in range(B_REP):
            rs = slice(i * BLK, (i + 1) * BLK)
            z = jnp.where(ok_c, s[rs] - slope[i] * dist_c, NEG)
            z_ref[rs, :ncmp] = z
            m_ref[rs, :] = _lanes(jnp.max(z, axis=-1, keepdims=True))

        @pl.when(b >= 0)
        def _():
            for i in range(B_REP):
                rs = slice(i * BLK, (i + 1) * BLK)
                e = jnp.where(ok_c, jnp.exp(z_ref[rs, :ncmp] - _tile_lanes(m_ref[rs, :], ncmp)), 0.0)
                z_ref[rs, :ncmp] = e
                a_ref[rs, :] = _lanes(jnp.maximum(jnp.sum(e, axis=-1, keepdims=True), TINY))

        @pl.when(b >= 0)
        def _():
            psum = jnp.zeros((BLK, ncmp), F32)
            for i in range(B_REP):
                rs = slice(i * BLK, (i + 1) * BLK)
                p = z_ref[rs, :ncmp] / _tile_lanes(a_ref[rs, :], ncmp)
                psum = psum + p
                p_ref[rs, :ncmp] = p.astype(BF16)
            oc_ref[g] = jnp.dot(p_ref[:, :ncmp], vc_ref[...], preferred_element_type=F32)
            ps_ref[g] = psum

    imps = []
    for g in range(B_KV):
        psum = ps_ref[g]
        p_hi = psum.astype(BF16)
        p_lo = (psum - p_hi.astype(F32)).astype(BF16)
        imps.append(lax.dot_general(ovt_ref[...], p_hi, NT_DIMS, preferred_element_type=F32)
                    + lax.dot_general(ovt_ref[...], p_lo, NT_DIMS, preferred_element_type=F32))
    imp = jnp.concatenate(imps, axis=1)
    jj = lax.broadcasted_iota(jnp.int32, (nslc, B_KV * BLK), 0)
    jf = jj.astype(F32)
    cur = jnp.right_shift(jnp.concatenate([tq_lane] * B_KV, axis=1), SLC_SHIFT)
    forced = (jj == 0) | (jj == cur) | (jj == cur - 1)
    score = jnp.where(forced, FORCE, jnp.where(jj <= cur, imp, NEG))
    sel_t = jnp.zeros((nslc, B_KV * BLK), F32)
    for _ in range(min(N_SEL, nslc)):
        mx = jnp.max(score, axis=0, keepdims=True)
        idx = jnp.min(jnp.where(score == mx, jf, 1e9), axis=0, keepdims=True)
        pick = jf == idx
        sel_t = jnp.where(pick & (mx > NEG / 2), 1.0, sel_t)
        score = jnp.where(pick, REMOVED, score)
    ntp = max(8, ntile_all)
    tile_of = jnp.right_shift(lax.broadcasted_iota(jnp.int32, (ntp, nslc), 1), SEL_KT_SHIFT - SLC_SHIFT)
    krow = lax.broadcasted_iota(jnp.int32, (ntp, nslc), 0)
    gmat = jnp.where(tile_of == krow, 1.0, 0.0).astype(BF16)
    cnt = jnp.dot(gmat, sel_t.astype(BF16), preferred_element_type=F32)
    pow2 = jnp.left_shift(1, lax.broadcasted_iota(jnp.int32, (ntp, 1), 0)).astype(F32)
    sel_bs, bitss = [], []
    for g in range(B_KV):
        gl = slice(g * BLK, (g + 1) * BLK)
        sel_bs.append(sel_t[:, gl].T.astype(BF16))
        anyq = jnp.max(cnt[:, gl], axis=-1, keepdims=True)
        bitss.append(jnp.sum(jnp.where(anyq > 0.5, pow2, 0.0)).astype(jnp.int32))

    for g in range(B_KV):
        half, slope, sel_b, bits = halves[g], slopes[g], sel_bs[g], bitss[g]

        wstart = pl.multiple_of(jnp.maximum(t0 - WIN, 0), BLK)
        uw = wstart + lax.broadcasted_iota(jnp.int32, (1, WIN_KEYS), 1)
        dw = tq - uw
        ok_w = (dw >= 0) & (dw < WIN)
        dist_w = (t0 + BLK - 1 - uw).astype(F32)
        s = lax.dot_general(qg_ref[g], kw_ref[pl.ds(wstart, WIN_KEYS), :], NT_DIMS, preferred_element_type=F32)
        for i in range(B_REP):
            rs = slice(i * BLK, (i + 1) * BLK)
            z = jnp.where(ok_w, s[rs] - slope[i] * dist_w, NEG)
            z_ref[rs, :] = z
            m_ref[rs, :] = _lanes(jnp.max(z, axis=-1, keepdims=True))

        @pl.when(b >= 0)
        def _():
            for i in range(B_REP):
                rs = slice(i * BLK, (i + 1) * BLK)
                p_ref[rs, :] = jnp.exp(z_ref[rs, :] - _tile_lanes(m_ref[rs, :], WIN_KEYS)).astype(BF16)
            vw = jnp.where(half, vw_ref[pl.ds(wstart, WIN_KEYS), :], one)
            ow = jnp.dot(p_ref[...], vw, preferred_element_type=F32)
            ow_ref[...] = ow / pltpu.roll(ow, B_HD, axis=1)

        m_ref[...] = jnp.full((rows, 2 * B_HD), NEG, F32)
        acc_ref[...] = jnp.zeros((rows, 2 * B_HD), F32)
        ntiles = (t0 + BLK + SEL_KT - 1) // SEL_KT

        def tile_body(it, carry):
            kt = ntiles - 1 - it

            @pl.when(jnp.bitwise_and(jnp.right_shift(bits, kt), 1) == 1)
            def _():
                k0 = pl.multiple_of(kt * SEL_KT, SEL_KT)
                us = k0 + lax.broadcasted_iota(jnp.int32, (1, SEL_KT), 1)
                selx = jnp.dot(sel_b, ex_ref[kt], preferred_element_type=F32)
                ok_s = (selx > 0.5) & (us <= tq)
                dist_s = (t0 + BLK - 1 - us).astype(F32)
                s = lax.dot_general(qg_ref[g], ks_ref[pl.ds(k0, SEL_KT), :], NT_DIMS, preferred_element_type=F32)
                for i in range(B_REP):
                    rs = slice(i * BLK, (i + 1) * BLK)
                    z = jnp.where(ok_s, s[rs] - slope[i] * dist_s, NEG)
                    z_ref[rs, :SEL_KT] = z
                    m_old = m_ref[rs, :]
                    m_new = jnp.maximum(m_old, _lanes(jnp.max(z, axis=-1, keepdims=True)))
                    a_ref[rs, :] = jnp.exp(m_old - m_new)
                    m_ref[rs, :] = m_new

                @pl.when(kt >= 0)
                def _():
                    for i in range(B_REP):
                        rs = slice(i * BLK, (i + 1) * BLK)
                        p_ref[rs, :SEL_KT] = jnp.exp(z_ref[rs, :SEL_KT]
                                                     - _tile_lanes(m_ref[rs, :], SEL_KT)).astype(BF16)
                    vs = jnp.where(half, vs_ref[pl.ds(k0, SEL_KT), :], one)
                    acc_ref[...] = a_ref[...] * acc_ref[...] + jnp.dot(p_ref[:, :SEL_KT], vs,
                                                                       preferred_element_type=F32)

            return carry

        lax.fori_loop(0, ntiles, tile_body, 0)

        for i in range(B_REP):
            rs = slice(i * BLK, (i + 1) * BLK)
            h = g * B_REP + i
            acc = acc_ref[rs, :]
            o_s = acc / pltpu.roll(acc, B_HD, axis=1)
            out = (gates[:, 3 * h:3 * h + 1] * oc_ref[g, rs, :] + gates[:, 3 * h + 1:3 * h + 2] * o_s
                   + gates[:, 3 * h + 2:3 * h + 3] * ow_ref[rs, :])
            cs = slice(i * 2 * B_HD, (i + 1) * 2 * B_HD)
            if g == 0:
                o_ref[:, cs] = out.astype(o_ref.dtype)
            else:
                o_ref[:, cs] = jnp.where(lo, o_ref[:, cs], out.astype(o_ref.dtype))


def _nsa_attention(proj, kc, vc, slopes):
    t = proj.shape[0]
    assert t % SEL_KT == 0 and SLC_LEN == 1 << SLC_SHIFT and SEL_KT == 1 << SEL_KT_SHIFT
    ncmp = kc.shape[0]
    nslc = t // SLC_LEN
    ntile = t // SEL_KT
    assert ntile <= 24
    c_start = jnp.arange(ncmp)[None, :] * CMP_STRIDE
    s_start = jnp.arange(nslc)[:, None] * SLC_LEN
    overlap = jnp.clip(jnp.minimum(c_start + CMP_LEN, s_start + SLC_LEN) - jnp.maximum(c_start, s_start),
                       0, None).astype(F32) / CMP_LEN
    overlap = jnp.where(jnp.arange(ncmp)[None, :] < ncmp - 1, overlap, 0.0).astype(BF16)
    key_blk = (jnp.arange(ntile)[:, None, None] * SEL_KT + jnp.arange(SEL_KT)[None, None, :]) // SLC_LEN
    expand = (key_blk == jnp.arange(nslc)[None, :, None]).astype(BF16)
    w2 = 2 * B_HD
    qcol = COL_BQ // (4 * w2)
    kvcol = COL_BKV // w2
    full = lambda c: pl.BlockSpec((t, w2), lambda b: (0, c))
    rows = B_REP * BLK
    return pl.pallas_call(
        _nsa_kernel,
        grid=(t // BLK,),
        in_specs=[pl.BlockSpec(memory_space=pltpu.SMEM),
                  pl.BlockSpec((BLK, 4 * w2), lambda b: (b, qcol)),
                  pl.BlockSpec((BLK, 4 * w2), lambda b: (b, qcol + 1)),
                  pl.BlockSpec((BLK, w2), lambda b: (b, COL_BG // w2)),
                  pl.BlockSpec((ncmp, w2), lambda b: (0, 0)),
                  pl.BlockSpec((ncmp, w2), lambda b: (0, 0)),
                  full(kvcol + 2), full(kvcol + 3), full(kvcol + 4), full(kvcol + 5),
                  pl.BlockSpec((nslc, ncmp), lambda b: (0, 0)),
                  pl.BlockSpec((ntile, nslc, SEL_KT), lambda b: (0, 0, 0))],
        out_specs=pl.BlockSpec((BLK, B_HEADS * B_HD), lambda b: (b, 0)),
        out_shape=jax.ShapeDtypeStruct((t, B_HEADS * B_HD), BF16),
        scratch_shapes=[pltpu.VMEM((B_KV, rows, w2), BF16),
                        pltpu.VMEM((B_KV, rows, w2), F32),
                        pltpu.VMEM((rows, w2), F32),
                        pltpu.VMEM((rows, w2), F32),
                        pltpu.VMEM((rows, w2), F32),
                        pltpu.VMEM((rows, w2), F32),
                        pltpu.VMEM((rows, WIN_KEYS), F32),
                        pltpu.VMEM((rows, WIN_KEYS), BF16),
                        pltpu.VMEM((B_KV, BLK, ncmp), F32)],
        compiler_params=_cparams(("parallel",)),
        name="nsa_attention",
    )(slopes, proj, proj, proj, kc, vc, proj, proj, proj, proj, overlap, expand)


MERGE_KC = 512


def _merge_kernel(o0_ref, o1_ref, o2_ref, l0_ref, l1_ref, l2_ref, yb_ref, x_ref, gn_ref, wg0_ref, wg1_ref,
                  wa_ref, wb_ref, wo_ref, h_ref, ya_ref, xn_ref):
    kc = pl.program_id(1)

    @pl.when(kc == 0)
    def _():
        for hd in range(A_HEADS):
            l0, l1, l2 = l0_ref[hd], l1_ref[hd], l2_ref[hd]
            m = jnp.maximum(jnp.maximum(l0, l1), l2)
            e0, e1, e2 = jnp.exp(l0 - m), jnp.exp(l1 - m), jnp.exp(l2 - m)
            den = e0 + e1 + e2
            ya = (e0 / den) * o0_ref[hd] + (e1 / den) * o1_ref[hd] + (e2 / den) * o2_ref[hd]
            ya_ref[:, hd * A_HD:(hd + 1) * A_HD] = ya.astype(BF16)
        x = x_ref[...]
        h_ref[...] = x
        xn_ref[...] = _rms(x, gn_ref[...]).astype(BF16)

    g0 = _sigmoid(jnp.dot(xn_ref[...], wg0_ref[...], preferred_element_type=F32))
    g1 = _sigmoid(jnp.dot(xn_ref[...], wg1_ref[...], preferred_element_type=F32))
    ua = jnp.dot(ya_ref[...], wa_ref[...], preferred_element_type=F32)
    ub = jnp.dot(yb_ref[...], wb_ref[...], preferred_element_type=F32)
    mix = g0 * ua + g1 * ub
    h_ref[...] += jnp.dot(mix.astype(BF16), wo_ref[...], preferred_element_type=F32)


def _merge(x, norm_mix, w_gate, dil_o, dil_lse, yb, wa, wb, wo, tm):
    t, d = x.shape
    nk = d // MERGE_KC
    row = lambda w: pl.BlockSpec((tm, w), lambda i, k: (i, 0))
    return pl.pallas_call(
        _merge_kernel,
        grid=(t // tm, nk),
        in_specs=[pl.BlockSpec((A_HEADS, tm, A_HD), lambda i, k: (0, i, 0))] * 6 + [row(B_HEADS * B_HD), row(d),
                  pl.BlockSpec((1, d), lambda i, k: (0, 0)),
                  pl.BlockSpec((d, MERGE_KC), lambda i, k: (0, k)),
                  pl.BlockSpec((d, MERGE_KC), lambda i, k: (0, nk + k)),
                  pl.BlockSpec((A_OUT, MERGE_KC), lambda i, k: (0, k)),
                  pl.BlockSpec((B_HEADS * B_HD, MERGE_KC), lambda i, k: (0, k)),
                  pl.BlockSpec((MERGE_KC, d), lambda i, k: (k, 0))],
        out_specs=pl.BlockSpec((tm, d), lambda i, k: (i, 0)),
        out_shape=jax.ShapeDtypeStruct((t, d), F32),
        scratch_shapes=[pltpu.VMEM((tm, A_OUT), BF16), pltpu.VMEM((tm, d), BF16)],
        compiler_params=_cparams(("parallel", "arbitrary")),
        name="merge",
    )(*dil_o, *dil_lse, yb, x, norm_mix.reshape(1, d), w_gate, w_gate, wa, wb, wo)


def _cross_kernel(h_ref, gx_ref, gf_ref, wq_ref, kv_ref, wo_ref, h2_ref, xn_ref):
    h = h_ref[...]
    hn = _rms(h, gx_ref[...]).astype(BF16)
    q = jnp.dot(hn, wq_ref[...], preferred_element_type=F32).astype(BF16)
    kv = kv_ref[...]
    nh = X_HEADS * X_HD
    outs = []
    for hd in range(X_HEADS):
        sl = slice(hd * X_HD, (hd + 1) * X_HD)
        s = lax.dot_general(q[:, sl], kv[:, sl], NT_DIMS, preferred_element_type=F32) * (X_HD ** -0.5)
        m = jnp.max(s, axis=-1, keepdims=True)
        p = jnp.exp(s - m)
        p = p / jnp.sum(p, axis=-1, keepdims=True)
        outs.append(jnp.dot(p.astype(BF16), kv[:, nh + hd * X_HD:nh + (hd + 1) * X_HD],
                            preferred_element_type=F32))
    o = jnp.concatenate(outs, axis=1).astype(BF16)
    h2 = h + jnp.dot(o, wo_ref[...], preferred_element_type=F32)
    h2_ref[...] = h2
    xn_ref[...] = _rms(h2, gf_ref[...]).astype(BF16)


def _cross(h1, norm_x, norm_ffn, wq, kv, wo, tm):
    t, d = h1.shape
    nm = kv.shape[0]
    nh = X_HEADS * X_HD
    const = lambda shape: pl.BlockSpec(shape, lambda i: (0, 0))
    return pl.pallas_call(
        _cross_kernel,
        grid=(t // tm,),
        in_specs=[pl.BlockSpec((tm, d), lambda i: (i, 0)), const((1, d)), const((1, d)),
                  const((d, nh)), const((nm, 2 * nh)), const((nh, d))],
        out_specs=[pl.BlockSpec((tm, d), lambda i: (i, 0)), pl.BlockSpec((tm, d), lambda i: (i, 0))],
        out_shape=[jax.ShapeDtypeStruct((t, d), F32), jax.ShapeDtypeStruct((t, d), BF16)],
        compiler_params=_cparams(("parallel",)),
        name="cross_attn",
    )(h1, norm_x.reshape(1, d), norm_ffn.reshape(1, d), wq, kv, wo)


ROUTE_TB = 128
ROUTE_HPI = 4
MAP_TOK = 16


def _top16_rows(s, rowf, nrows):
    vals, idxs = [], []
    for _ in range(PEER_TOPK):
        mx = jnp.max(s, axis=0, keepdims=True)
        idx = jnp.min(jnp.where(s == mx, rowf, float(nrows)), axis=0, keepdims=True)
        vals.append(mx)
        idxs.append(idx)
        s = jnp.where(rowf == idx, REMOVED, s)
    return vals, idxs


def _route_kernel(xn_ref, wq_ref, k1_ref, k2_ref, w_ref, q_scr, g_scr, e1_scr, e2_scr, gt_scr, e1t_scr, e2t_scr,
                  wt_scr):
    tb = xn_ref.shape[0]
    half = PEER_DK // 2
    q = jnp.dot(xn_ref[...], wq_ref[...], preferred_element_type=F32).astype(BF16)
    for c in range(2 * PEER_HEADS):
        q_scr[c] = q[:, c * half:(c + 1) * half]
    nl = ROUTE_HPI * tb
    rowf = lax.broadcasted_iota(jnp.int32, (N_KEYS, nl), 0).astype(F32)
    nc = 8 * 10
    rowc = lax.broadcasted_iota(jnp.int32, (nc, nl), 0).astype(F32)

    def head_body(hp, carry):
        h0 = hp * ROUTE_HPI
        s1 = jnp.concatenate([lax.dot_general(k1_ref[...], q_scr[2 * (h0 + u)], NT_DIMS, preferred_element_type=F32)
                              for u in range(ROUTE_HPI)], axis=1)
        s2 = jnp.concatenate([lax.dot_general(k2_ref[...], q_scr[2 * (h0 + u) + 1], NT_DIMS,
                                              preferred_element_type=F32) for u in range(ROUTE_HPI)], axis=1)
        v1, i1 = _top16_rows(s1, rowf, N_KEYS)
        v2, i2 = _top16_rows(s2, rowf, N_KEYS)
        sub8 = lax.broadcasted_iota(jnp.int32, (8, nl), 0)
        v2lo, v2hi = jnp.concatenate(v2[:8], axis=0), jnp.concatenate(v2[8:], axis=0)
        i2lo, i2hi = jnp.concatenate(i2[:8], axis=0), jnp.concatenate(i2[8:], axis=0)
        cand_p = [v1[0] + v2lo, v1[0] + v2hi]
        e1_p = [jnp.broadcast_to(i1[0], (8, nl))] * 2
        e2_p = [i2lo, i2hi]
        for a in range(1, 8):
            nb = PEER_TOPK // (a + 1)
            piece = v1[a] + v2lo
            cand_p.append(piece if nb >= 8 else jnp.where(sub8 < nb, piece, REMOVED))
            e1_p.append(jnp.broadcast_to(i1[a], (8, nl)))
            e2_p.append(i2lo)
        cand_p.append(jnp.concatenate(v1[8:], axis=0) + v2[0])
        e1_p.append(jnp.concatenate(i1[8:], axis=0))
        e2_p.append(jnp.broadcast_to(i2[0], (8, nl)))
        cand = jnp.concatenate(cand_p, axis=0)
        e1c = jnp.concatenate(e1_p, axis=0)
        e2c = jnp.concatenate(e2_p, axis=0)
        vs, e1, e2 = [], [], []
        for _ in range(PEER_TOPK):
            mx = jnp.max(cand, axis=0, keepdims=True)
            idx = jnp.min(jnp.where(cand == mx, rowc, float(nc)), axis=0, keepdims=True)
            pick = rowc == idx
            vs.append(mx)
            e1.append(jnp.sum(jnp.where(pick, e1c, 0.0), axis=0, keepdims=True))
            e2.append(jnp.sum(jnp.where(pick, e2c, 0.0), axis=0, keepdims=True))
            cand = jnp.where(pick, REMOVED, cand)
        vsa = jnp.concatenate(vs, axis=0)
        ex = jnp.exp(vsa - vs[0])
        gate = ex / jnp.sum(ex, axis=0, keepdims=True)
        e1a, e2a = jnp.concatenate(e1, axis=0), jnp.concatenate(e2, axis=0)
        for u in range(ROUTE_HPI):
            r0 = pl.multiple_of((h0 + u) * PEER_TOPK, PEER_TOPK)
            ls = slice(u * tb, (u + 1) * tb)
            g_scr[pl.ds(r0, PEER_TOPK), :] = gate[:, ls]
            e1_scr[pl.ds(r0, PEER_TOPK), :] = e1a[:, ls]
            e2_scr[pl.ds(r0, PEER_TOPK), :] = e2a[:, ls]
        return carry

    lax.fori_loop(0, PEER_HEADS // ROUTE_HPI, head_body, 0)

    gt_scr[...] = g_scr[...].T
    e1t_scr[...] = e1_scr[...].T
    e2t_scr[...] = e2_scr[...].T
    sub = lax.broadcasted_iota(jnp.int32, (N_KEYS, PEER_HEADS * PEER_TOPK), 0).astype(F32)

    def group_body(grp, carry):
        shape = (N_KEYS, PEER_HEADS * PEER_TOPK)
        for tt in range(MAP_TOK):
            t = grp * MAP_TOK + tt
            g_row = jnp.broadcast_to(gt_scr[pl.ds(t, 1), :], shape)
            e1_row = jnp.broadcast_to(e1t_scr[pl.ds(t, 1), :], shape)
            e2_row = jnp.broadcast_to(e2t_scr[pl.ds(t, 1), :], shape)
            a = jnp.where(e1_row == sub, g_row, 0.0).astype(BF16)
            bm = jnp.where(e2_row == sub, 1.0, 0.0).astype(BF16)
            wt_scr[pl.ds(tt, N_KEYS, stride=MAP_TOK), :] = lax.dot_general(a, bm, NT_DIMS,
                                                                          preferred_element_type=F32)
        w_ref[grp] = wt_scr[...].astype(w_ref.dtype)
        return carry

    lax.fori_loop(0, tb // MAP_TOK, group_body, 0)


def _peer_route(xn, wq, k1, k2):
    t, d = xn.shape
    tb = ROUTE_TB
    half = PEER_DK // 2
    slots = PEER_HEADS * PEER_TOPK
    const = lambda shape: pl.BlockSpec(shape, lambda i: (0,) * len(shape))
    return pl.pallas_call(
        _route_kernel,
        grid=(t // tb,),
        in_specs=[pl.BlockSpec((tb, d), lambda i: (i, 0)), const((d, PEER_HEADS * PEER_DK)),
                  const((N_KEYS, half)), const((N_KEYS, half))],
        out_specs=pl.BlockSpec((tb // MAP_TOK, N_KEYS * MAP_TOK, N_KEYS), lambda i: (i, 0, 0)),
        out_shape=jax.ShapeDtypeStruct((t // MAP_TOK, N_KEYS * MAP_TOK, N_KEYS), BF16),
        scratch_shapes=[pltpu.VMEM((2 * PEER_HEADS, tb, half), BF16),
                        pltpu.VMEM((slots, tb), F32), pltpu.VMEM((slots, tb), F32), pltpu.VMEM((slots, tb), F32),
                        pltpu.VMEM((tb, slots), F32), pltpu.VMEM((tb, slots), F32), pltpu.VMEM((tb, slots), F32),
                        pltpu.VMEM((N_KEYS * MAP_TOK, N_KEYS), F32)],
        compiler_params=_cparams(("parallel",)),
        name="peer_route",
    )(xn, wq, k1, k2)


def _castT_kernel(u_ref, o_ref):
    o_ref[...] = u_ref[...].T.astype(o_ref.dtype)


def _cast_transpose(u, te, td):
    ne, d = u.shape
    return pl.pallas_call(
        _castT_kernel,
        grid=(ne // te, d // td),
        in_specs=[pl.BlockSpec((te, td), lambda i, j: (i, j))],
        out_specs=pl.BlockSpec((td, te), lambda i, j: (j, i)),
        out_shape=jax.ShapeDtypeStruct((d, ne), BF16),
        compiler_params=_cparams(("parallel", "parallel")),
        name="expert_u_transpose",
    )(u)


def _dense_kernel(xn_ref, ut_ref, w_ref, v_ref, h_ref, gf_ref, o_ref, acc_ref, aw_ref):
    e = pl.program_id(1)

    @pl.when(e == 0)
    def _():
        acc_ref[...] = jnp.zeros_like(acc_ref)

    tb, eb = aw_ref.shape
    a = _gelu(lax.dot_general(xn_ref[...], ut_ref[...], NT_DIMS, preferred_element_type=F32))
    for bb in range(tb // MAP_TOK):
        rs = slice(bb * MAP_TOK, (bb + 1) * MAP_TOK)
        for j in range(eb // N_KEYS):
            cs = slice(j * N_KEYS, (j + 1) * N_KEYS)
            aw_ref[rs, cs] = (a[rs, cs] * w_ref[bb, j * MAP_TOK:(j + 1) * MAP_TOK, :].astype(F32)).astype(BF16)
    acc_ref[...] += jnp.dot(aw_ref[...], v_ref[...], preferred_element_type=F32)

    @pl.when(e == pl.num_programs(1) - 1)
    def _():
        o_ref[...] = _rms(h_ref[...] + acc_ref[...], gf_ref[...]).astype(o_ref.dtype)


def _peer_dense(xn, ut, wmap, v, h2, norm_final, tb, eb):
    t, d = xn.shape
    ne = ut.shape[0]
    return pl.pallas_call(
        _dense_kernel,
        grid=(t // tb, ne // eb),
        in_specs=[pl.BlockSpec((tb, d), lambda i, e: (i, 0)),
                  pl.BlockSpec((eb, d), lambda i, e: (e, 0)),
                  pl.BlockSpec((tb // MAP_TOK, eb // N_KEYS * MAP_TOK, N_KEYS), lambda i, e: (i, e, 0)),
                  pl.BlockSpec((eb, d), lambda i, e: (e, 0)),
                  pl.BlockSpec((tb, d), lambda i, e: (i, 0)),
                  pl.BlockSpec((1, d), lambda i, e: (0, 0))],
        out_specs=pl.BlockSpec((tb, d), lambda i, e: (i, 0)),
        out_shape=jax.ShapeDtypeStruct((t, d), F32),
        scratch_shapes=[pltpu.VMEM((tb, d), F32), pltpu.VMEM((tb, eb), BF16)],
        compiler_params=_cparams(("parallel", "arbitrary")),
        name="peer_dense",
    )(xn, ut, wmap, v, h2, norm_final.reshape(1, d))


def _alibi_slopes(n):
    return jnp.power(2.0, -8.0 * (jnp.arange(n, dtype=F32) + 1.0) / n)


def _block(x, mem, norm_mix, w_in, w_cmp_k1, w_cmp_k2, pe_cmp_k, w_cmp_v1, w_cmp_v2, pe_cmp_v,
           w_up_a, w_up_b, w_out, norm_x, norm_mem, w_xq, w_xkv, w_xo, norm_ffn, w_pq,
           sub_keys1, sub_keys2, expert_u, expert_v, norm_out, out_dtype):
    t, d = x.shape
    assert d == D_MODEL and w_in.shape[1] == COL_BG + B_GATE_COLS + 2 * D_MODEL
    tm = min(1024, t)

    proj, *strided = _in_projection(x, norm_mix, w_in, tm)
    w_gate = _gate_weights(w_in, COL_BG + B_GATE_COLS, MERGE_KC)

    slopes_a = _alibi_slopes(A_GROUPS * A_HEADS)
    dil = [_dilated_group(src, slopes_a, g, t) for g, src in enumerate([proj] + strided)]

    kc, vc = _nsa_compress(proj, pe_cmp_k, pe_cmp_v, w_cmp_k1, w_cmp_k2, w_cmp_v1, w_cmp_v2)
    yb = _nsa_attention(proj, kc, vc, _alibi_slopes(B_HEADS))

    wb = w_up_b.reshape(B_KV, B_REP, B_HD, d).transpose(1, 0, 2, 3).reshape(B_HEADS * B_HD, d)
    h1 = _merge(x, norm_mix, w_gate, [o for o, _ in dil], [l for _, l in dil], yb,
                w_up_a.astype(BF16), wb.astype(BF16), w_out.astype(BF16), min(512, t))

    kvm = _norm_matmul(mem, norm_mem, w_xkv.astype(BF16), mem.shape[0], 512, BF16)
    h2, xn = _cross(h1, norm_x, norm_ffn, w_xq.astype(BF16), kvm, w_xo.astype(BF16), min(512, t))

    wmap = _peer_route(xn, w_pq.astype(BF16), sub_keys1.astype(BF16), sub_keys2.astype(BF16))
    return _peer_dense(xn, expert_u.astype(BF16), wmap, expert_v.astype(BF16), h2, norm_out,
                       min(512, t), 1024).astype(out_dtype)


@jax.jit
def kernel(x, mem, norm_mix, w_in, w_cmp_k1, w_cmp_k2, pe_cmp_k, w_cmp_v1, w_cmp_v2, pe_cmp_v, w_up_a, w_up_b,
           w_out, norm_x, norm_mem, w_xq, w_xkv, w_xo, norm_ffn, w_pq, sub_keys1, sub_keys2, expert_u,
           expert_v, norm_final):
    assert x.shape[0] == 1 and mem.shape[0] == 1 and norm_mix.shape[0] == 1
    out = _block(x[0], mem[0], norm_mix[0], w_in[0], w_cmp_k1[0], w_cmp_k2[0], pe_cmp_k[0], w_cmp_v1[0],
                 w_cmp_v2[0], pe_cmp_v[0], w_up_a[0], w_up_b[0], w_out[0], norm_x[0], norm_mem[0], w_xq[0],
                 w_xkv[0], w_xo[0], norm_ffn[0], w_pq[0], sub_keys1[0], sub_keys2[0], expert_u[0],
                 expert_v[0], norm_final, x.dtype)
    return out[None]
```

```python
import functools

import jax
import jax.numpy as jnp
from jax import lax
from jax.experimental import pallas as pl
from jax.experimental.pallas import tpu as pltpu

F32 = jnp.float32
BF16 = jnp.bfloat16

D_MODEL = 2048
BLK = 128
EPS = 1e-6
NEG = -1e30
TINY = 1e-30
FORCE = 1e4
REMOVED = -3e38

A_DILATIONS = (1, 4, 16)
A_WINDOWS = (128, 512, 2048)
A_GROUPS = 3
A_HEADS = 4
A_HD = 128
B_HEADS = 16
B_KV = 2
B_REP = B_HEADS // B_KV
B_HD = 64
CMP_LEN = 32
CMP_STRIDE = 16
CMP_HIDDEN = 128
SLC_LEN = 64
SLC_SHIFT = 6
N_SEL = 16
WIN = 512
X_HEADS = 4
X_HD = 128
PEER_HEADS = 8
PEER_DK = 256
N_KEYS = 128
N_EXPERTS = N_KEYS * N_KEYS
PEER_TOPK = 16

A_QKV_COLS = 3 * A_GROUPS * A_HEADS * A_HD
A_OUT = A_HEADS * A_HD
B_Q_COLS = B_HEADS * B_HD
B_KV_COLS = 3 * 2 * B_KV * B_HD
B_GATE_COLS = B_HEADS * 3

COL_A = 0
COL_BQ = COL_A + A_QKV_COLS
COL_BKV = COL_BQ + B_Q_COLS
COL_BG = COL_BKV + B_KV_COLS
PROJ_COLS = 6656

VMEM_LIMIT = 56 * 1024 * 1024

NT_DIMS = (((1,), (1,)), ((), ()))


def _cparams(sem):
    return pltpu.CompilerParams(dimension_semantics=sem, vmem_limit_bytes=VMEM_LIMIT)


def _rms(x, g):
    ms = jnp.mean(x * x, axis=-1, keepdims=True)
    return x * lax.rsqrt(ms + EPS) * g


_GELU_K1 = -2.0 * 0.7978845608028654 * 1.4426950408889634
_GELU_K2 = _GELU_K1 * 0.044715


def _gelu(x):
    return x / (1.0 + jnp.exp2(x * (_GELU_K1 + _GELU_K2 * (x * x))))


def _sigmoid(x):
    return 1.0 / (1.0 + jnp.exp(-x))


def _nmm_kernel(x_ref, g_ref, w_ref, o_ref, xn_ref):
    @pl.when(pl.program_id(1) == 0)
    def _():
        xn_ref[...] = _rms(x_ref[...].astype(F32), g_ref[...]).astype(BF16)

    o_ref[...] = jnp.dot(xn_ref[...], w_ref[...], preferred_element_type=F32).astype(o_ref.dtype)


def _norm_matmul(x, gain, w, tm, tn, out_dtype):
    t, d = x.shape
    n = w.shape[1]
    return pl.pallas_call(
        _nmm_kernel,
        grid=(t // tm, n // tn),
        in_specs=[pl.BlockSpec((tm, d), lambda i, j: (i, 0)),
                  pl.BlockSpec((1, d), lambda i, j: (0, 0)),
                  pl.BlockSpec((d, tn), lambda i, j: (0, j))],
        out_specs=pl.BlockSpec((tm, tn), lambda i, j: (i, j)),
        out_shape=jax.ShapeDtypeStruct((t, n), out_dtype),
        scratch_shapes=[pltpu.VMEM((tm, d), BF16)],
        compiler_params=_cparams(("parallel", "arbitrary")),
        name="norm_matmul",
    )(x, gain.reshape(1, d), w)


IN_TN = A_OUT


def _inproj_kernel(x_ref, g_ref, w_ref, proj_ref, *rest):
    dil_refs, (xn_ref, res_ref) = rest[:-2], rest[-2:]
    j = pl.program_id(1)
    tm = x_ref.shape[0]

    @pl.when(j == 0)
    def _():
        xn_ref[...] = _rms(x_ref[...], g_ref[...]).astype(BF16)

    res = jnp.dot(xn_ref[...], w_ref[...], preferred_element_type=F32)
    proj_ref[...] = res.astype(proj_ref.dtype)
    for out_ref, group in zip(dil_refs, range(1, A_GROUPS)):
        dil = A_DILATIONS[group]
        for s in range(3):
            @pl.when(j == s * A_GROUPS + group)
            def _(out_ref=out_ref, dil=dil, s=s):
                for c in range(IN_TN // A_HD):
                    res_ref[c] = res[:, c * A_HD:(c + 1) * A_HD]
                for r in range(dil):
                    for c in range(IN_TN // A_HD):
                        c0 = (r * 3 + s) * IN_TN + c * A_HD
                        out_ref[:, c0:c0 + A_HD] = res_ref[c, pl.ds(r, tm // dil, stride=dil), :].astype(out_ref.dtype)


def _in_projection(x, gain, w_in, tm):
    t, d = x.shape
    groups = range(1, A_GROUPS)
    assert all(tm % A_DILATIONS[g] == 0 and (tm // A_DILATIONS[g]) % 16 == 0 for g in groups)
    dil_specs = [pl.BlockSpec((tm // A_DILATIONS[g], A_DILATIONS[g] * 3 * A_OUT), lambda i, j: (i, 0)) for g in groups]
    dil_shapes = [jax.ShapeDtypeStruct((t // A_DILATIONS[g], A_DILATIONS[g] * 3 * A_OUT), BF16) for g in groups]
    return pl.pallas_call(
        _inproj_kernel,
        grid=(t // tm, PROJ_COLS // IN_TN),
        in_specs=[pl.BlockSpec((tm, d), lambda i, j: (i, 0)),
                  pl.BlockSpec((1, d), lambda i, j: (0, 0)),
                  pl.BlockSpec((d, IN_TN), lambda i, j: (0, j))],
        out_specs=[pl.BlockSpec((tm, IN_TN), lambda i, j: (i, j))] + dil_specs,
        out_shape=[jax.ShapeDtypeStruct((t, PROJ_COLS), BF16)] + dil_shapes,
        scratch_shapes=[pltpu.VMEM((tm, d), BF16), pltpu.VMEM((IN_TN // A_HD, tm, A_HD), F32)],
        compiler_params=_cparams(("parallel", "arbitrary")),
        name="in_projection",
    )(x, gain.reshape(1, d), w_in)


def _gate_w_kernel(a_ref, b_ref, o_ref, *, shift):
    both = jnp.concatenate([a_ref[...], b_ref[...]], axis=0)
    o_ref[...] = both[shift:shift + o_ref.shape[1], :].T.astype(o_ref.dtype)


def _gate_weights(w_t, first, tn):
    n_in, d = w_t.shape
    n = n_in - first
    blk0, shift = divmod(first, tn)
    assert n % tn == 0 and shift % 8 == 0 and 0 < shift and (blk0 + n // tn) * tn + shift == n_in
    return pl.pallas_call(
        functools.partial(_gate_w_kernel, shift=shift),
        grid=(n // tn,),
        in_specs=[pl.BlockSpec((tn, d), lambda j: (blk0 + j, 0)),
                  pl.BlockSpec((tn, d), lambda j: (blk0 + j + 1, 0))],
        out_specs=pl.BlockSpec((d, tn), lambda j: (0, j)),
        out_shape=jax.ShapeDtypeStruct((d, n), BF16),
        compiler_params=_cparams(("parallel",)),
        name="gate_weights",
    )(w_t, w_t)


def _dil_kernel(slopes_ref, q_ref, kp_ref, ko_ref, vp_ref, vo_ref, o_ref, lse_ref, z_ref, m_ref, *, dil, qb, group):
    b = pl.program_id(0)
    rows = pl.ds(pl.program_id(1), qb, stride=dil) if dil > 1 else slice(None)
    nk = qb + BLK
    q = q_ref[...]
    k = jnp.concatenate([kp_ref[...], ko_ref[...]], axis=0)
    i = lax.broadcasted_iota(jnp.int32, (qb, nk), 0)
    j = lax.broadcasted_iota(jnp.int32, (qb, nk), 1)
    rel = i + BLK - j
    jmin = jnp.where(b > 0, 0, BLK)
    ok = (rel >= 0) & (rel <= BLK) & (j >= jmin)
    dist = (rel * dil).astype(F32)
    scale = A_HD ** -0.5
    for h in range(A_HEADS):
        sl = slice(h * A_HD, (h + 1) * A_HD)
        s = lax.dot_general(q[:, sl], k[:, sl], NT_DIMS, preferred_element_type=F32) * scale
        s = jnp.where(ok, s - slopes_ref[group * A_HEADS + h] * dist, NEG)
        z_ref[h] = s
        m_ref[h] = jnp.broadcast_to(jnp.max(s, axis=-1, keepdims=True), (qb, A_HD))

    @pl.when(b >= 0)
    def _():
        v = jnp.concatenate([vp_ref[...], vo_ref[...]], axis=0)
        for h in range(A_HEADS):
            sl = slice(h * A_HD, (h + 1) * A_HD)
            m = m_ref[h]
            p = jnp.exp(z_ref[h] - jnp.concatenate([m] * (nk // A_HD), axis=1))
            den = jnp.sum(p, axis=-1, keepdims=True)
            o_ref[h, rows, :] = jnp.dot(p.astype(BF16), v[:, sl], preferred_element_type=F32) / den
            lse_ref[h, rows, :] = m + jnp.log(den)


def _dilated_group(src, slopes, group, t):
    dil = A_DILATIONS[group]
    assert A_WINDOWS[group] // dil == BLK
    l = t // dil
    qb = min(256 if dil < 16 else BLK, l)
    assert l % qb == 0 and qb % BLK == 0
    stride, first, step = (PROJ_COLS // A_OUT, group, A_GROUPS) if dil == 1 else (3, 0, 1)

    def col(s):
        return lambda b, r: (b, r * stride + first + s * step)

    def colp(s):
        return lambda b, r: (jnp.maximum(b * (qb // BLK) - 1, 0), r * stride + first + s * step)

    out_spec = pl.BlockSpec((A_HEADS, qb * dil, A_HD), lambda b, r: (0, b, 0))
    out_shape = jax.ShapeDtypeStruct((A_HEADS, t, A_HD), F32)
    return pl.pallas_call(
        functools.partial(_dil_kernel, dil=dil, qb=qb, group=group),
        grid=(l // qb, dil),
        in_specs=[pl.BlockSpec(memory_space=pltpu.SMEM),
                  pl.BlockSpec((qb, A_OUT), col(0)),
                  pl.BlockSpec((BLK, A_OUT), colp(1)),
                  pl.BlockSpec((qb, A_OUT), col(1)),
                  pl.BlockSpec((BLK, A_OUT), colp(2)),
                  pl.BlockSpec((qb, A_OUT), col(2))],
        out_specs=[out_spec, out_spec],
        out_shape=[out_shape, out_shape],
        scratch_shapes=[pltpu.VMEM((A_HEADS, qb, qb + BLK), F32), pltpu.VMEM((A_HEADS, qb, A_HD), F32)],
        compiler_params=_cparams(("parallel", "arbitrary")),
        name=f"dilated_attn_g{group}",
    )(slopes, src, src, src, src, src)


def _cmp_kernel(a_ref, pe_ref, w1_ref, w2_ref, o_ref):
    a = a_ref[0]
    n16 = a.shape[0]
    nxt = pltpu.roll(a, n16 - 1, axis=0)
    blocks = jnp.concatenate([a, nxt], axis=1) + pe_ref[0]
    hid = _gelu(jnp.dot(blocks.astype(BF16), w1_ref[0], preferred_element_type=F32))
    out = jnp.dot(hid.astype(BF16), w2_ref[0], preferred_element_type=F32)
    row = lax.broadcasted_iota(jnp.int32, out.shape, 0)
    o_ref[0] = jnp.where(row < n16 - 1, out, 0.0)


def _nsa_compress(proj, pe_k, pe_v, w_k1, w_k2, w_v1, w_v2):
    t = proj.shape[0]
    assert CMP_LEN == 2 * CMP_STRIDE and t % CMP_STRIDE == 0
    n16 = t // CMP_STRIDE
    kv = proj[:, COL_BKV:COL_BKV + 2 * B_KV * B_HD].astype(F32)
    a4 = kv.reshape(t, 2 * B_KV, B_HD).transpose(1, 0, 2).reshape(2 * B_KV, n16, CMP_STRIDE * B_HD)
    pe = jnp.stack([pe_k.reshape(1, -1), pe_v.reshape(1, -1)])
    w1 = jnp.stack([w_k1, w_v1]).astype(BF16)
    w2 = jnp.stack([w_k2, w_v2]).astype(BF16)
    kd = CMP_LEN * B_HD
    out = pl.pallas_call(
        _cmp_kernel,
        grid=(2 * B_KV,),
        in_specs=[pl.BlockSpec((1, n16, CMP_STRIDE * B_HD), lambda i: (i, 0, 0)),
                  pl.BlockSpec((1, 1, kd), lambda i: (i // B_KV, 0, 0)),
                  pl.BlockSpec((1, kd, CMP_HIDDEN), lambda i: (i // B_KV, 0, 0)),
                  pl.BlockSpec((1, CMP_HIDDEN, B_HD), lambda i: (i // B_KV, 0, 0))],
        out_specs=pl.BlockSpec((1, n16, B_HD), lambda i: (i, 0, 0)),
        out_shape=jax.ShapeDtypeStruct((2 * B_KV, n16, B_HD), F32),
        compiler_params=_cparams(("parallel",)),
        name="nsa_compress",
    )(a4, pe, w1, w2)
    kc = jnp.concatenate([out[0], out[1]], axis=-1).astype(BF16)
    vc = jnp.concatenate([out[2], out[3]], axis=-1).astype(BF16)
    return kc, vc


SEL_KT = 512
SEL_KT_SHIFT = 9
WIN_KEYS = WIN + BLK


def _lanes(col):
    return jnp.broadcast_to(col, (col.shape[0], 2 * B_HD))


def _tile_lanes(rep, n):
    return jnp.concatenate([rep] * (n // (2 * B_HD)), axis=1)


def _nsa_kernel(slopes_ref, q0_ref, q1_ref, gate_ref, kc_ref, vc_ref, ks_ref, vs_ref, kw_ref, vw_ref,
                ovt_ref, ex_ref, o_ref, qg_ref, oc_ref, ow_ref, acc_ref, m_ref, a_ref, z_ref, p_ref, ps_ref):
    b = pl.program_id(0)
    t0 = b * BLK
    rows = B_REP * BLK
    ncmp = kc_ref.shape[0]
    nslc = ovt_ref.shape[0]
    ntile_all = ex_ref.shape[0]
    lane = lax.broadcasted_iota(jnp.int32, (1, 2 * B_HD), 1)
    lo = lane < B_HD
    tq = t0 + lax.broadcasted_iota(jnp.int32, (BLK, 1), 0)
    tq_lane = t0 + lax.broadcasted_iota(jnp.int32, (1, BLK), 1)
    qscale = jnp.asarray(B_HD ** -0.5, BF16)
    qnat = [q0_ref[...] * qscale, q1_ref[...] * qscale]
    gates = _sigmoid(gate_ref[...].astype(F32))
    zero = jnp.zeros((), BF16)
    one = jnp.ones((), BF16)

    cend = lax.broadcasted_iota(jnp.int32, (1, ncmp), 1) * CMP_STRIDE + (CMP_LEN - 1)
    ok_c = cend <= tq
    dist_c = (t0 + BLK - 1 - cend).astype(F32)
    halves = [lo, jnp.logical_not(lo)]
    slopes = [[slopes_ref[g * B_REP + i] for i in range(B_REP)] for g in range(B_KV)]

    for g in range(B_KV):
        slope = slopes[g]
        for i in range(B_REP):
            x = qnat[g][:, (i // 2) * 2 * B_HD:(i // 2 + 1) * 2 * B_HD]
            if i % 2 != g:
                x = jnp.concatenate([x[:, B_HD:], x[:, :B_HD]], axis=1)
            qg_ref[g, i * BLK:(i + 1) * BLK, :] = jnp.where(halves[g], x, zero)
        s = lax.dot_general(qg_ref[g], kc_ref[...], NT_DIMS, preferred_element_type=F32)
        for i in range(B_REP):
            rs = slice(i * BLK, (i + 1) * BLK)
            z = jnp.where(ok_c, s[rs] - slope[i] * dist_c, NEG)
            z_ref[rs, :ncmp] = z
            m_ref[rs, :] = _lanes(jnp.max(z, axis=-1, keepdims=True))

        @pl.when(b >= 0)
        def _():
            for i in range(B_REP):
                rs = slice(i * BLK, (i + 1) * BLK)
                e = jnp.where(ok_c, jnp.exp(z_ref[rs, :ncmp] - _tile_lanes(m_ref[rs, :], ncmp)), 0.0)
                z_ref[rs, :ncmp] = e
                a_ref[rs, :] = _lanes(jnp.maximum(jnp.sum(e, axis=-1, keepdims=True), TINY))

        @pl.when(b >= 0)
        def _():
            psum = jnp.zeros((BLK, ncmp), F32)
            for i in range(B_REP):
                rs = slice(i * BLK, (i + 1) * BLK)
                p = z_ref[rs, :ncmp] / _tile_lanes(a_ref[rs, :], ncmp)
                psum = psum + p
                p_ref[rs, :ncmp] = p.astype(BF16)
            oc_ref[g] = jnp.dot(p_ref[:, :ncmp], vc_ref[...], preferred_element_type=F32)
            ps_ref[g] = psum

    imps = []
    for g in range(B_KV):
        psum = ps_ref[g]
        p_hi = psum.astype(BF16)
        p_lo = (psum - p_hi.astype(F32)).astype(BF16)
        imps.append(lax.dot_general(ovt_ref[...], p_hi, NT_DIMS, preferred_element_type=F32)
                    + lax.dot_general(ovt_ref[...], p_lo, NT_DIMS, preferred_element_type=F32))
    imp = jnp.concatenate(imps, axis=1)
    jj = lax.broadcasted_iota(jnp.int32, (nslc, B_KV * BLK), 0)
    jf = jj.astype(F32)
    cur = jnp.right_shift(jnp.concatenate([tq_lane] * B_KV, axis=1), SLC_SHIFT)
    forced = (jj == 0) | (jj == cur) | (jj == cur - 1)
    score = jnp.where(forced, FORCE, jnp.where(jj <= cur, imp, NEG))
    sel_t = jnp.zeros((nslc, B_KV * BLK), F32)
    for _ in range(min(N_SEL, nslc)):
        mx = jnp.max(score, axis=0, keepdims=True)
        idx = jnp.min(jnp.where(score == mx, jf, 1e9), axis=0, keepdims=True)
        pick = jf == idx
        sel_t = jnp.where(pick & (mx > NEG / 2), 1.0, sel_t)
        score = jnp.where(pick, REMOVED, score)
    ntp = max(8, ntile_all)
    tile_of = jnp.right_shift(lax.broadcasted_iota(jnp.int32, (ntp, nslc), 1), SEL_KT_SHIFT - SLC_SHIFT)
    krow = lax.broadcasted_iota(jnp.int32, (ntp, nslc), 0)
    gmat = jnp.where(tile_of == krow, 1.0, 0.0).astype(BF16)
    cnt = jnp.dot(gmat, sel_t.astype(BF16), preferred_element_type=F32)
    pow2 = jnp.left_shift(1, lax.broadcasted_iota(jnp.int32, (ntp, 1), 0)).astype(F32)
    sel_bs, bitss = [], []
    for g in range(B_KV):
        gl = slice(g * BLK, (g + 1) * BLK)
        sel_bs.append(sel_t[:, gl].T.astype(BF16))
        anyq = jnp.max(cnt[:, gl], axis=-1, keepdims=True)
        bitss.append(jnp.sum(jnp.where(anyq > 0.5, pow2, 0.0)).astype(jnp.int32))

    for g in range(B_KV):
        half, slope, sel_b, bits = halves[g], slopes[g], sel_bs[g], bitss[g]

        wstart = pl.multiple_of(jnp.maximum(t0 - WIN, 0), BLK)
        uw = wstart + lax.broadcasted_iota(jnp.int32, (1, WIN_KEYS), 1)
        dw = tq - uw
        ok_w = (dw >= 0) & (dw < WIN)
        dist_w = (t0 + BLK - 1 - uw).astype(F32)
        s = lax.dot_general(qg_ref[g], kw_ref[pl.ds(wstart, WIN_KEYS), :], NT_DIMS, preferred_element_type=F32)
        for i in range(B_REP):
            rs = slice(i * BLK, (i + 1) * BLK)
            z = jnp.where(ok_w, s[rs] - slope[i] * dist_w, NEG)
            z_ref[rs, :] = z
            m_ref[rs, :] = _lanes(jnp.max(z, axis=-1, keepdims=True))

        @pl.when(b >= 0)
        def _():
            for i in range(B_REP):
                rs = slice(i * BLK, (i + 1) * BLK)
                p_ref[rs, :] = jnp.exp(z_ref[rs, :] - _tile_lanes(m_ref[rs, :], WIN_KEYS)).astype(BF16)
            vw = jnp.where(half, vw_ref[pl.ds(wstart, WIN_KEYS), :], one)
            ow = jnp.dot(p_ref[...], vw, preferred_element_type=F32)
            ow_ref[...] = ow / pltpu.roll(ow, B_HD, axis=1)

        m_ref[...] = jnp.full((rows, 2 * B_HD), NEG, F32)
        acc_ref[...] = jnp.zeros((rows, 2 * B_HD), F32)
        ntiles = (t0 + BLK + SEL_KT - 1) // SEL_KT

        def tile_body(it, carry):
            kt = ntiles - 1 - it

            @pl.when(jnp.bitwise_and(jnp.right_shift(bits, kt), 1) == 1)
            def _():
                k0 = pl.multiple_of(kt * SEL_KT, SEL_KT)
                us = k0 + lax.broadcasted_iota(jnp.int32, (1, SEL_KT), 1)
                selx = jnp.dot(sel_b, ex_ref[kt], preferred_element_type=F32)
                ok_s = (selx > 0.5) & (us <= tq)
                dist_s = (t0 + BLK - 1 - us).astype(F32)
                s = lax.dot_general(qg_ref[g], ks_ref[pl.ds(k0, SEL_KT), :], NT_DIMS, preferred_element_type=F32)
                for i in range(B_REP):
                    rs = slice(i * BLK, (i + 1) * BLK)
                    z = jnp.where(ok_s, s[rs] - slope[i] * dist_s, NEG)
                    z_ref[rs, :SEL_KT] = z
                    m_old = m_ref[rs, :]
                    m_new = jnp.maximum(m_old, _lanes(jnp.max(z, axis=-1, keepdims=True)))
                    a_ref[rs, :] = jnp.exp(m_old - m_new)
                    m_ref[rs, :] = m_new

                @pl.when(kt >= 0)
                def _():
                    for i in range(B_REP):
                        rs = slice(i * BLK, (i + 1) * BLK)
                        p_ref[rs, :SEL_KT] = jnp.exp(z_ref[rs, :SEL_KT]
                                                     - _tile_lanes(m_ref[rs, :], SEL_KT)).astype(BF16)
                    vs = jnp.where(half, vs_ref[pl.ds(k0, SEL_KT), :], one)
                    acc_ref[...] = a_ref[...] * acc_ref[...] + jnp.dot(p_ref[:, :SEL_KT], vs,
                                                                       preferred_element_type=F32)

            return carry

        lax.fori_loop(0, ntiles, tile_body, 0)

        for i in range(B_REP):
            rs = slice(i * BLK, (i + 1) * BLK)
            h = g * B_REP + i
            acc = acc_ref[rs, :]
            o_s = acc / pltpu.roll(acc, B_HD, axis=1)
            out = (gates[:, 3 * h:3 * h + 1] * oc_ref[g, rs, :] + gates[:, 3 * h + 1:3 * h + 2] * o_s
                   + gates[:, 3 * h + 2:3 * h + 3] * ow_ref[rs, :])
            cs = slice(i * 2 * B_HD, (i + 1) * 2 * B_HD)
            if g == 0:
                o_ref[:, cs] = out.astype(o_ref.dtype)
            else:
                o_ref[:, cs] = jnp.where(lo, o_ref[:, cs], out.astype(o_ref.dtype))


def _nsa_attention(proj, kc, vc, slopes):
    t = proj.shape[0]
    assert t % SEL_KT == 0 and SLC_LEN == 1 << SLC_SHIFT and SEL_KT == 1 << SEL_KT_SHIFT
    ncmp = kc.shape[0]
    nslc = t // SLC_LEN
    ntile = t // SEL_KT
    assert ntile <= 24
    c_start = jnp.arange(ncmp)[None, :] * CMP_STRIDE
    s_start = jnp.arange(nslc)[:, None] * SLC_LEN
    overlap = jnp.clip(jnp.minimum(c_start + CMP_LEN, s_start + SLC_LEN) - jnp.maximum(c_start, s_start),
                       0, None).astype(F32) / CMP_LEN
    overlap = jnp.where(jnp.arange(ncmp)[None, :] < ncmp - 1, overlap, 0.0).astype(BF16)
    key_blk = (jnp.arange(ntile)[:, None, None] * SEL_KT + jnp.arange(SEL_KT)[None, None, :]) // SLC_LEN
    expand = (key_blk == jnp.arange(nslc)[None, :, None]).astype(BF16)
    w2 = 2 * B_HD
    qcol = COL_BQ // (4 * w2)
    kvcol = COL_BKV // w2
    full = lambda c: pl.BlockSpec((t, w2), lambda b: (0, c))
    rows = B_REP * BLK
    return pl.pallas_call(
        _nsa_kernel,
        grid=(t // BLK,),
        in_specs=[pl.BlockSpec(memory_space=pltpu.SMEM),
                  pl.BlockSpec((BLK, 4 * w2), lambda b: (b, qcol)),
                  pl.BlockSpec((BLK, 4 * w2), lambda b: (b, qcol + 1)),
                  pl.BlockSpec((BLK, w2), lambda b: (b, COL_BG // w2)),
                  pl.BlockSpec((ncmp, w2), lambda b: (0, 0)),
                  pl.BlockSpec((ncmp, w2), lambda b: (0, 0)),
                  full(kvcol + 2), full(kvcol + 3), full(kvcol + 4), full(kvcol + 5),
                  pl.BlockSpec((nslc, ncmp), lambda b: (0, 0)),
                  pl.BlockSpec((ntile, nslc, SEL_KT), lambda b: (0, 0, 0))],
        out_specs=pl.BlockSpec((BLK, B_HEADS * B_HD), lambda b: (b, 0)),
        out_shape=jax.ShapeDtypeStruct((t, B_HEADS * B_HD), BF16),
        scratch_shapes=[pltpu.VMEM((B_KV, rows, w2), BF16),
                        pltpu.VMEM((B_KV, rows, w2), F32),
                        pltpu.VMEM((rows, w2), F32),
                        pltpu.VMEM((rows, w2), F32),
                        pltpu.VMEM((rows, w2), F32),
                        pltpu.VMEM((rows, w2), F32),
                        pltpu.VMEM((rows, WIN_KEYS), F32),
                        pltpu.VMEM((rows, WIN_KEYS), BF16),
                        pltpu.VMEM((B_KV, BLK, ncmp), F32)],
        compiler_params=_cparams(("parallel",)),
        name="nsa_attention",
    )(slopes, proj, proj, proj, kc, vc, proj, proj, proj, proj, overlap, expand)


MERGE_KC = 512


def _merge_kernel(o0_ref, o1_ref, o2_ref, l0_ref, l1_ref, l2_ref, yb_ref, x_ref, gn_ref, wg0_ref, wg1_ref,
                  wa_ref, wb_ref, wo_ref, h_ref, ya_ref, xn_ref):
    kc = pl.program_id(1)

    @pl.when(kc == 0)
    def _():
        for hd in range(A_HEADS):
            l0, l1, l2 = l0_ref[hd], l1_ref[hd], l2_ref[hd]
            m = jnp.maximum(jnp.maximum(l0, l1), l2)
            e0, e1, e2 = jnp.exp(l0 - m), jnp.exp(l1 - m), jnp.exp(l2 - m)
            den = e0 + e1 + e2
            ya = (e0 / den) * o0_ref[hd] + (e1 / den) * o1_ref[hd] + (e2 / den) * o2_ref[hd]
            ya_ref[:, hd * A_HD:(hd + 1) * A_HD] = ya.astype(BF16)
        x = x_ref[...]
        h_ref[...] = x
        xn_ref[...] = _rms(x, gn_ref[...]).astype(BF16)

    g0 = _sigmoid(jnp.dot(xn_ref[...], wg0_ref[...], preferred_element_type=F32))
    g1 = _sigmoid(jnp.dot(xn_ref[...], wg1_ref[...], preferred_element_type=F32))
    ua = jnp.dot(ya_ref[...], wa_ref[...], preferred_element_type=F32)
    ub = jnp.dot(yb_ref[...], wb_ref[...], preferred_element_type=F32)
    mix = g0 * ua + g1 * ub
    h_ref[...] += jnp.dot(mix.astype(BF16), wo_ref[...], preferred_element_type=F32)


def _merge(x, norm_mix, w_gate, dil_o, dil_lse, yb, wa, wb, wo, tm):
    t, d = x.shape
    nk = d // MERGE_KC
    row = lambda w: pl.BlockSpec((tm, w), lambda i, k: (i, 0))
    return pl.pallas_call(
        _merge_kernel,
        grid=(t // tm, nk),
        in_specs=[pl.BlockSpec((A_HEADS, tm, A_HD), lambda i, k: (0, i, 0))] * 6 + [row(B_HEADS * B_HD), row(d),
                  pl.BlockSpec((1, d), lambda i, k: (0, 0)),
                  pl.BlockSpec((d, MERGE_KC), lambda i, k: (0, k)),
                  pl.BlockSpec((d, MERGE_KC), lambda i, k: (0, nk + k)),
                  pl.BlockSpec((A_OUT, MERGE_KC), lambda i, k: (0, k)),
                  pl.BlockSpec((B_HEADS * B_HD, MERGE_KC), lambda i, k: (0, k)),
                  pl.BlockSpec((MERGE_KC, d), lambda i, k: (k, 0))],
        out_specs=pl.BlockSpec((tm, d), lambda i, k: (i, 0)),
        out_shape=jax.ShapeDtypeStruct((t, d), F32),
        scratch_shapes=[pltpu.VMEM((tm, A_OUT), BF16), pltpu.VMEM((tm, d), BF16)],
        compiler_params=_cparams(("parallel", "arbitrary")),
        name="merge",
    )(*dil_o, *dil_lse, yb, x, norm_mix.reshape(1, d), w_gate, w_gate, wa, wb, wo)


def _cross_kernel(h_ref, gx_ref, gf_ref, wq_ref, kv_ref, wo_ref, h2_ref, xn_ref):
    h = h_ref[...]
    hn = _rms(h, gx_ref[...]).astype(BF16)
    q = jnp.dot(hn, wq_ref[...], preferred_element_type=F32).astype(BF16)
    kv = kv_ref[...]
    nh = X_HEADS * X_HD
    outs = []
    for hd in range(X_HEADS):
        sl = slice(hd * X_HD, (hd + 1) * X_HD)
        s = lax.dot_general(q[:, sl], kv[:, sl], NT_DIMS, preferred_element_type=F32) * (X_HD ** -0.5)
        m = jnp.max(s, axis=-1, keepdims=True)
        p = jnp.exp(s - m)
        p = p / jnp.sum(p, axis=-1, keepdims=True)
        outs.append(jnp.dot(p.astype(BF16), kv[:, nh + hd * X_HD:nh + (hd + 1) * X_HD],
                            preferred_element_type=F32))
    o = jnp.concatenate(outs, axis=1).astype(BF16)
    h2 = h + jnp.dot(o, wo_ref[...], preferred_element_type=F32)
    h2_ref[...] = h2
    xn_ref[...] = _rms(h2, gf_ref[...]).astype(BF16)


def _cross(h1, norm_x, norm_ffn, wq, kv, wo, tm):
    t, d = h1.shape
    nm = kv.shape[0]
    nh = X_HEADS * X_HD
    const = lambda shape: pl.BlockSpec(shape, lambda i: (0, 0))
    return pl.pallas_call(
        _cross_kernel,
        grid=(t // tm,),
        in_specs=[pl.BlockSpec((tm, d), lambda i: (i, 0)), const((1, d)), const((1, d)),
                  const((d, nh)), const((nm, 2 * nh)), const((nh, d))],
        out_specs=[pl.BlockSpec((tm, d), lambda i: (i, 0)), pl.BlockSpec((tm, d), lambda i: (i, 0))],
        out_shape=[jax.ShapeDtypeStruct((t, d), F32), jax.ShapeDtypeStruct((t, d), BF16)],
        compiler_params=_cparams(("parallel",)),
        name="cross_attn",
    )(h1, norm_x.reshape(1, d), norm_ffn.reshape(1, d), wq, kv, wo)


ROUTE_TB = 128
ROUTE_HPI = 4
MAP_TOK = 16


def _top16_rows(s, rowf, nrows):
    vals, idxs = [], []
    for _ in range(PEER_TOPK):
        mx = jnp.max(s, axis=0, keepdims=True)
        idx = jnp.min(jnp.where(s == mx, rowf, float(nrows)), axis=0, keepdims=True)
        vals.append(mx)
        idxs.append(idx)
        s = jnp.where(rowf == idx, REMOVED, s)
    return vals, idxs


def _route_kernel(xn_ref, wq_ref, k1_ref, k2_ref, w_ref, q_scr, g_scr, e1_scr, e2_scr, gt_scr, e1t_scr, e2t_scr,
                  wt_scr):
    tb = xn_ref.shape[0]
    half = PEER_DK // 2
    q = jnp.dot(xn_ref[...], wq_ref[...], preferred_element_type=F32).astype(BF16)
    for c in range(2 * PEER_HEADS):
        q_scr[c] = q[:, c * half:(c + 1) * half]
    nl = ROUTE_HPI * tb
    rowf = lax.broadcasted_iota(jnp.int32, (N_KEYS, nl), 0).astype(F32)
    nc = 8 * 10
    rowc = lax.broadcasted_iota(jnp.int32, (nc, nl), 0).astype(F32)

    def head_body(hp, carry):
        h0 = hp * ROUTE_HPI
        s1 = jnp.concatenate([lax.dot_general(k1_ref[...], q_scr[2 * (h0 + u)], NT_DIMS, preferred_element_type=F32)
                              for u in range(ROUTE_HPI)], axis=1)
        s2 = jnp.concatenate([lax.dot_general(k2_ref[...], q_scr[2 * (h0 + u) + 1], NT_DIMS,
                                              preferred_element_type=F32) for u in range(ROUTE_HPI)], axis=1)
        v1, i1 = _top16_rows(s1, rowf, N_KEYS)
        v2, i2 = _top16_rows(s2, rowf, N_KEYS)
        sub8 = lax.broadcasted_iota(jnp.int32, (8, nl), 0)
        v2lo, v2hi = jnp.concatenate(v2[:8], axis=0), jnp.concatenate(v2[8:], axis=0)
        i2lo, i2hi = jnp.concatenate(i2[:8], axis=0), jnp.concatenate(i2[8:], axis=0)
        cand_p = [v1[0] + v2lo, v1[0] + v2hi]
        e1_p = [jnp.broadcast_to(i1[0], (8, nl))] * 2
        e2_p = [i2lo, i2hi]
        for a in range(1, 8):
            nb = PEER_TOPK // (a + 1)
            piece = v1[a] + v2lo
            cand_p.append(piece if nb >= 8 else jnp.where(sub8 < nb, piece, REMOVED))
            e1_p.append(jnp.broadcast_to(i1[a], (8, nl)))
            e2_p.append(i2lo)
        cand_p.append(jnp.concatenate(v1[8:], axis=0) + v2[0])
        e1_p.append(jnp.concatenate(i1[8:], axis=0))
        e2_p.append(jnp.broadcast_to(i2[0], (8, nl)))
        cand = jnp.concatenate(cand_p, axis=0)
        e1c = jnp.concatenate(e1_p, axis=0)
        e2c = jnp.concatenate(e2_p, axis=0)
        vs, e1, e2 = [], [], []
        for _ in range(PEER_TOPK):
            mx = jnp.max(cand, axis=0, keepdims=True)
            idx = jnp.min(jnp.where(cand == mx, rowc, float(nc)), axis=0, keepdims=True)
            pick = rowc == idx
            vs.append(mx)
            e1.append(jnp.sum(jnp.where(pick, e1c, 0.0), axis=0, keepdims=True))
            e2.append(jnp.sum(jnp.where(pick, e2c, 0.0), axis=0, keepdims=True))
            cand = jnp.where(pick, REMOVED, cand)
        vsa = jnp.concatenate(vs, axis=0)
        ex = jnp.exp(vsa - vs[0])
        gate = ex / jnp.sum(ex, axis=0, keepdims=True)
        e1a, e2a = jnp.concatenate(e1, axis=0), jnp.concatenate(e2, axis=0)
        for u in range(ROUTE_HPI):
            r0 = pl.multiple_of((h0 + u) * PEER_TOPK, PEER_TOPK)
            ls = slice(u * tb, (u + 1) * tb)
            g_scr[pl.ds(r0, PEER_TOPK), :] = gate[:, ls]
            e1_scr[pl.ds(r0, PEER_TOPK), :] = e1a[:, ls]
            e2_scr[pl.ds(r0, PEER_TOPK), :] = e2a[:, ls]
        return carry

    lax.fori_loop(0, PEER_HEADS // ROUTE_HPI, head_body, 0)

    gt_scr[...] = g_scr[...].T
    e1t_scr[...] = e1_scr[...].T
    e2t_scr[...] = e2_scr[...].T
    sub = lax.broadcasted_iota(jnp.int32, (N_KEYS, PEER_HEADS * PEER_TOPK), 0).astype(F32)

    def group_body(grp, carry):
        shape = (N_KEYS, PEER_HEADS * PEER_TOPK)
        for tt in range(MAP_TOK):
            t = grp * MAP_TOK + tt
            g_row = jnp.broadcast_to(gt_scr[pl.ds(t, 1), :], shape)
            e1_row = jnp.broadcast_to(e1t_scr[pl.ds(t, 1), :], shape)
            e2_row = jnp.broadcast_to(e2t_scr[pl.ds(t, 1), :], shape)
            a = jnp.where(e1_row == sub, g_row, 0.0).astype(BF16)
            bm = jnp.where(e2_row == sub, 1.0, 0.0).astype(BF16)
            wt_scr[pl.ds(tt, N_KEYS, stride=MAP_TOK), :] = lax.dot_general(a, bm, NT_DIMS,
                                                                          preferred_element_type=F32)
        w_ref[grp] = wt_scr[...].astype(w_ref.dtype)
        return carry

    lax.fori_loop(0, tb // MAP_TOK, group_body, 0)


def _peer_route(xn, wq, k1, k2):
    t, d = xn.shape
    tb = ROUTE_TB
    half = PEER_DK // 2
    slots = PEER_HEADS * PEER_TOPK
    const = lambda shape: pl.BlockSpec(shape, lambda i: (0,) * len(shape))
    return pl.pallas_call(
        _route_kernel,
        grid=(t // tb,),
        in_specs=[pl.BlockSpec((tb, d), lambda i: (i, 0)), const((d, PEER_HEADS * PEER_DK)),
                  const((N_KEYS, half)), const((N_KEYS, half))],
        out_specs=pl.BlockSpec((tb // MAP_TOK, N_KEYS * MAP_TOK, N_KEYS), lambda i: (i, 0, 0)),
        out_shape=jax.ShapeDtypeStruct((t // MAP_TOK, N_KEYS * MAP_TOK, N_KEYS), BF16),
        scratch_shapes=[pltpu.VMEM((2 * PEER_HEADS, tb, half), BF16),
                        pltpu.VMEM((slots, tb), F32), pltpu.VMEM((slots, tb), F32), pltpu.VMEM((slots, tb), F32),
                        pltpu.VMEM((tb, slots), F32), pltpu.VMEM((tb, slots), F32), pltpu.VMEM((tb, slots), F32),
                        pltpu.VMEM((N_KEYS * MAP_TOK, N_KEYS), F32)],
        compiler_params=_cparams(("parallel",)),
        name="peer_route",
    )(xn, wq, k1, k2)


def _castT_kernel(u_ref, o_ref):
    o_ref[...] = u_ref[...].T.astype(o_ref.dtype)


def _cast_transpose(u, te, td, rows=None, name="expert_u_transpose"):
    ne, d = u.shape
    ne = ne if rows is None else rows
    return pl.pallas_call(
        _castT_kernel,
        grid=(ne // te, d // td),
        in_specs=[pl.BlockSpec((te, td), lambda i, j: (i, j))],
        out_specs=pl.BlockSpec((td, te), lambda i, j: (j, i)),
        out_shape=jax.ShapeDtypeStruct((d, ne), BF16),
        compiler_params=_cparams(("parallel", "parallel")),
        name=name,
    )(u)


def _dense_kernel(xn_ref, ut_ref, w_ref, v_ref, h_ref, gf_ref, o_ref, acc_ref, aw_ref):
    e = pl.program_id(1)

    @pl.when(e == 0)
    def _():
        acc_ref[...] = jnp.zeros_like(acc_ref)

    tb, eb = aw_ref.shape
    a = _gelu(jnp.dot(xn_ref[...], ut_ref[...], preferred_element_type=F32))
    for bb in range(tb // MAP_TOK):
        rs = slice(bb * MAP_TOK, (bb + 1) * MAP_TOK)
        for j in range(eb // N_KEYS):
            cs = slice(j * N_KEYS, (j + 1) * N_KEYS)
            aw_ref[rs, cs] = (a[rs, cs] * w_ref[bb, j * MAP_TOK:(j + 1) * MAP_TOK, :].astype(F32)).astype(BF16)
    acc_ref[...] += jnp.dot(aw_ref[...], v_ref[...], preferred_element_type=F32)

    @pl.when(e == pl.num_programs(1) - 1)
    def _():
        o_ref[...] = _rms(h_ref[...] + acc_ref[...], gf_ref[...]).astype(o_ref.dtype)


def _peer_dense(xn, ut, wmap, v, h2, norm_final, tb, eb):
    t, d = xn.shape
    ne = ut.shape[1]
    return pl.pallas_call(
        _dense_kernel,
        grid=(t // tb, ne // eb),
        in_specs=[pl.BlockSpec((tb, d), lambda i, e: (i, 0)),
                  pl.BlockSpec((d, eb), lambda i, e: (0, e)),
                  pl.BlockSpec((tb // MAP_TOK, eb // N_KEYS * MAP_TOK, N_KEYS), lambda i, e: (i, e, 0)),
                  pl.BlockSpec((eb, d), lambda i, e: (e, 0)),
                  pl.BlockSpec((tb, d), lambda i, e: (i, 0)),
                  pl.BlockSpec((1, d), lambda i, e: (0, 0))],
        out_specs=pl.BlockSpec((tb, d), lambda i, e: (i, 0)),
        out_shape=jax.ShapeDtypeStruct((t, d), F32),
        scratch_shapes=[pltpu.VMEM((tb, d), F32), pltpu.VMEM((tb, eb), BF16)],
        compiler_params=_cparams(("parallel", "arbitrary")),
        name="peer_dense",
    )(xn, ut, wmap, v, h2, norm_final.reshape(1, d))


def _alibi_slopes(n):
    return jnp.power(2.0, -8.0 * (jnp.arange(n, dtype=F32) + 1.0) / n)


def _block(x, mem, norm_mix, w_in, w_cmp_k1, w_cmp_k2, pe_cmp_k, w_cmp_v1, w_cmp_v2, pe_cmp_v,
           w_up_a, w_up_b, w_out, norm_x, norm_mem, w_xq, w_xkv, w_xo, norm_ffn, w_pq,
           sub_keys1, sub_keys2, expert_u, expert_v, norm_out, out_dtype):
    t, d = x.shape
    assert d == D_MODEL and w_in.shape[1] == COL_BG + B_GATE_COLS + 2 * D_MODEL
    tm = min(1024, t)

    w_t = w_in.T
    w_mix = _cast_transpose(w_t, 512, 1024, rows=PROJ_COLS, name="mixer_weights")
    w_gate = _gate_weights(w_t, COL_BG + B_GATE_COLS, MERGE_KC)
    proj, *strided = _in_projection(x, norm_mix, w_mix, tm)

    slopes_a = _alibi_slopes(A_GROUPS * A_HEADS)
    dil = [_dilated_group(src, slopes_a, g, t) for g, src in enumerate([proj] + strided)]

    kc, vc = _nsa_compress(proj, pe_cmp_k, pe_cmp_v, w_cmp_k1, w_cmp_k2, w_cmp_v1, w_cmp_v2)
    yb = _nsa_attention(proj, kc, vc, _alibi_slopes(B_HEADS))

    wb = w_up_b.reshape(B_KV, B_REP, B_HD, d).transpose(1, 0, 2, 3).reshape(B_HEADS * B_HD, d)
    h1 = _merge(x, norm_mix, w_gate, [o for o, _ in dil], [l for _, l in dil], yb,
                w_up_a.astype(BF16), wb.astype(BF16), w_out.astype(BF16), min(512, t))

    kvm = _norm_matmul(mem, norm_mem, w_xkv.astype(BF16), mem.shape[0], 512, BF16)
    h2, xn = _cross(h1, norm_x, norm_ffn, w_xq.astype(BF16), kvm, w_xo.astype(BF16), min(512, t))

    wmap = _peer_route(xn, w_pq.astype(BF16), sub_keys1.astype(BF16), sub_keys2.astype(BF16))
    return _peer_dense(xn, _cast_transpose(expert_u, 512, 1024), wmap, expert_v.astype(BF16), h2, norm_out,
                       min(512, t), 1024).astype(out_dtype)


@jax.jit
def kernel(x, mem, norm_mix, w_in, w_cmp_k1, w_cmp_k2, pe_cmp_k, w_cmp_v1, w_cmp_v2, pe_cmp_v, w_up_a, w_up_b,
           w_out, norm_x, norm_mem, w_xq, w_xkv, w_xo, norm_ffn, w_pq, sub_keys1, sub_keys2, expert_u,
           expert_v, norm_final):
    assert x.shape[0] == 1 and mem.shape[0] == 1 and norm_mix.shape[0] == 1
    out = _block(x[0], mem[0], norm_mix[0], w_in[0], w_cmp_k1[0], w_cmp_k2[0], pe_cmp_k[0], w_cmp_v1[0],
                 w_cmp_v2[0], pe_cmp_v[0], w_up_a[0], w_up_b[0], w_out[0], norm_x[0], norm_mem[0], w_xq[0],
                 w_xkv[0], w_xo[0], norm_ffn[0], w_pq[0], sub_keys1[0], sub_keys2[0], expert_u[0],
                 expert_v[0], norm_final, x.dtype)
    return out[None]
```
